```python
import math
import jax, jax.numpy as jnp
from jax import lax
import numpy as np

D_MODEL = 1024
BATCH = 2
SEQ = 16384
DEPTH = 1
DEC_BATCH = 128
DEC_SEQ = 1
PAST_LEN = 8192
PAGE_SIZE = 128

HEAD_DIM = 64
N_HEADS_MOBA = 8
N_HEADS_SB = 8
W_MOBA = N_HEADS_MOBA * HEAD_DIM
W_SB = N_HEADS_SB * HEAD_DIM
D_MIX = W_MOBA + W_SB
D_PROJ = 3 * W_MOBA + 3 * W_SB
D_FF = 4 * D_MODEL
MOBA_BLOCK = 256
MOBA_TOPK = 3
Q_BLOCK = 128
EPS = 1e-6
ATTN_SCALE = HEAD_DIM ** -0.5
F32 = jnp.float32

kernel_name = "hymba_moba_stickbreak_step"


def _rms_norm(x, g):
    xf = x.astype(F32)
    y = xf * lax.rsqrt(jnp.mean(xf * xf, axis=-1, keepdims=True) + EPS)
    return (y * g.astype(F32)).astype(x.dtype)


def _alibi_slopes():
    return jnp.exp2(-8.0 * jnp.arange(1, N_HEADS_MOBA + 1, dtype=F32) / N_HEADS_MOBA)


def _project(x, g_attn, w_in, g_q, g_k):
    b, l, _ = x.shape
    h = _rms_norm(x, g_attn) @ w_in
    cuts = [W_MOBA, 2 * W_MOBA, 3 * W_MOBA, 3 * W_MOBA + W_SB, 3 * W_MOBA + 2 * W_SB]
    qa, ka, va, qb, kb, vb = jnp.split(h, cuts, axis=-1)
    qa = _rms_norm(qa.reshape(b, l, N_HEADS_MOBA, HEAD_DIM), g_q)
    ka = _rms_norm(ka.reshape(b, l, N_HEADS_MOBA, HEAD_DIM), g_k)
    va = va.reshape(b, l, N_HEADS_MOBA, HEAD_DIM)
    qb = qb.reshape(b, l, N_HEADS_SB, HEAD_DIM)
    kb = kb.reshape(b, l, N_HEADS_SB, HEAD_DIM)
    vb = vb.reshape(b, l, N_HEADS_SB, HEAD_DIM)
    return qa, ka, va, qb, kb, vb


def _merge_mlp(x, oa, ob, g_oa, g_ob, w_out, g_mlp, w_up, w_down):
    b, l, _ = x.shape
    mixed = jnp.concatenate([_rms_norm(oa.reshape(b, l, W_MOBA), g_oa),
                             _rms_norm(ob.reshape(b, l, W_SB), g_ob)], axis=-1)
    h = x + mixed @ w_out
    u = _rms_norm(h, g_mlp) @ w_up
    return h + jnp.square(jax.nn.relu(u)) @ w_down


def _moba_select(q, kmean, n_full):
    nb = kmean.shape[1]
    s = jnp.einsum("bqhd,bnhd->bqhn", q.astype(F32), kmean.astype(F32))
    is_past = jnp.arange(nb)[None, :] < n_full[:, None]
    s = jnp.where(is_past[None, :, None, :], s, -jnp.inf)
    if nb < MOBA_TOPK:
        s = jnp.pad(s, ((0, 0), (0, 0), (0, 0), (0, MOBA_TOPK - nb)), constant_values=-jnp.inf)
    _, sel = lax.top_k(s, MOBA_TOPK)
    sel = jnp.minimum(sel, nb - 1)
    valid = jnp.arange(MOBA_TOPK)[None, :] < n_full[:, None]
    return sel, valid


def _alibi_softmax(logits, qpos, kpos, mask, slopes):
    dist = (qpos[None, :, None, None] - kpos).astype(F32)
    z = logits.astype(F32) * ATTN_SCALE - slopes[None, None, :, None] * dist
    return jax.nn.softmax(jnp.where(mask, z, -jnp.inf), axis=-1)


def _moba_prompt(q, k, v, slopes):
    b, s, h, d = q.shape
    nb = -(-s // MOBA_BLOCK)
    pad = ((0, 0), (0, nb * MOBA_BLOCK - s), (0, 0), (0, 0))
    kp = jnp.pad(k, pad)
    vp = jnp.pad(v, pad)
    k_blocks = kp.reshape(b, nb, MOBA_BLOCK, h, d)
    kmean = jnp.mean(k_blocks.astype(F32), axis=2)
    k_bt = k_blocks.transpose(0, 3, 1, 2, 4)
    v_bt = vp.reshape(b, nb, MOBA_BLOCK, h, d).transpose(0, 3, 1, 2, 4)
    sel, valid = _moba_select(q, kmean, jnp.arange(s) // MOBA_BLOCK)
    nq = s // Q_BLOCK
    bidx = jnp.arange(b)[:, None, None, None]
    hidx = jnp.arange(h)[None, None, :, None]
    offs = jnp.arange(MOBA_BLOCK)
    n_sel = MOBA_TOPK * MOBA_BLOCK

    def block(args):
        qi, si, vi, i = args
        q0 = i * Q_BLOCK
        qpos = q0 + jnp.arange(Q_BLOCK)
        own0 = (q0 // MOBA_BLOCK) * MOBA_BLOCK
        ks = k_bt[bidx, hidx, si]
        vs = v_bt[bidx, hidx, si]
        ko = lax.dynamic_slice_in_dim(kp, own0, MOBA_BLOCK, axis=1)
        vo = lax.dynamic_slice_in_dim(vp, own0, MOBA_BLOCK, axis=1)
        l_sel = jnp.einsum("bqhd,bqhrkd->bqhrk", qi, ks, preferred_element_type=F32)
        l_own = jnp.einsum("bqhd,bkhd->bqhk", qi, ko, preferred_element_type=F32)
        logits = jnp.concatenate([l_sel.reshape(b, Q_BLOCK, h, n_sel), l_own], axis=-1)
        kpos_sel = (si[..., None] * MOBA_BLOCK + offs).reshape(b, Q_BLOCK, h, n_sel)
        kpos_own = jnp.broadcast_to(own0 + offs, (b, Q_BLOCK, h, MOBA_BLOCK))
        kpos = jnp.concatenate([kpos_sel, kpos_own], axis=-1)
        m_sel = jnp.repeat(vi, MOBA_BLOCK, axis=-1)
        m_own = (own0 + offs)[None, :] <= qpos[:, None]
        mask = jnp.concatenate([m_sel, m_own], axis=-1)[None, :, None, :]
        p = _alibi_softmax(logits, qpos, kpos, mask, slopes)
        out = (jnp.einsum("bqhrk,bqhrkd->bqhd",
                          p[..., :n_sel].reshape(b, Q_BLOCK, h, MOBA_TOPK, MOBA_BLOCK), vs.astype(F32))
               + jnp.einsum("bqhk,bkhd->bqhd", p[..., n_sel:], vo.astype(F32)))
        return out.astype(q.dtype)

    q_blocks = q.reshape(b, nq, Q_BLOCK, h, d).swapaxes(0, 1)
    sel_blocks = sel.reshape(b, nq, Q_BLOCK, h, MOBA_TOPK).swapaxes(0, 1)
    valid_blocks = valid.reshape(nq, Q_BLOCK, MOBA_TOPK)
    out = lax.map(block, (q_blocks, sel_blocks, valid_blocks, jnp.arange(nq)))
    return out.swapaxes(0, 1).reshape(b, s, h, d)


def _gather_rows(cache, layer, page_table, new, pos):
    db, t, h, _ = new.shape
    past = page_table.shape[1] * PAGE_SIZE
    bidx = jnp.arange(db)[:, None, None, None]
    hidx = jnp.arange(h)[None, None, :, None]
    pp = jnp.clip(pos, 0, past - 1)
    phys = page_table[bidx, pp // PAGE_SIZE]
    rows_past = cache[layer, phys, pp % PAGE_SIZE, hidx]
    rows_new = new[bidx, jnp.clip(pos - past, 0, t - 1), hidx]
    return jnp.where((pos < past)[..., None], rows_past, rows_new)


def _moba_sample(q, k_new, v_new, cache_k, cache_v, layer, page_table, slopes):
    db, t, h, d = q.shape
    n_pages = page_table.shape[1]
    past = n_pages * PAGE_SIZE
    nb = -(-(past + t) // MOBA_BLOCK)
    page_sum = cache_k[layer, page_table].astype(F32).sum(axis=2)
    page_blk = jax.nn.one_hot((jnp.arange(n_pages) * PAGE_SIZE) // MOBA_BLOCK, nb, dtype=F32)
    qpos = past + jnp.arange(t)
    new_blk = jax.nn.one_hot(qpos // MOBA_BLOCK, nb, dtype=F32)
    kmean = (jnp.einsum("bphd,pn->bnhd", page_sum, page_blk)
             + jnp.einsum("bthd,tn->bnhd", k_new.astype(F32), new_blk)) / MOBA_BLOCK
    sel, valid = _moba_select(q, kmean, qpos // MOBA_BLOCK)
    offs = jnp.arange(MOBA_BLOCK)
    own0 = (qpos // MOBA_BLOCK) * MOBA_BLOCK
    kpos_sel = (sel[..., None] * MOBA_BLOCK + offs).reshape(db, t, h, MOBA_TOPK * MOBA_BLOCK)
    kpos_own = jnp.broadcast_to((own0[:, None] + offs)[None, :, None, :], (db, t, h, MOBA_BLOCK))
    kpos = jnp.concatenate([kpos_sel, kpos_own], axis=-1)
    kg = _gather_rows(cache_k, layer, page_table, k_new, kpos)
    vg = _gather_rows(cache_v, layer, page_table, v_new, kpos)
    logits = jnp.einsum("bthd,bthkd->bthk", q, kg, preferred_element_type=F32)
    m_sel = jnp.repeat(valid, MOBA_BLOCK, axis=-1)
    m_own = (own0[:, None] + offs) <= qpos[:, None]
    mask = jnp.concatenate([m_sel, m_own], axis=-1)[None, :, None, :]
    p = _alibi_softmax(logits, qpos, kpos, mask, slopes)
    return jnp.einsum("bthk,bthkd->bthd", p, vg.astype(F32)).astype(q.dtype)


def _stick_breaking_weights(z, allowed):
    log_beta = jax.nn.log_sigmoid(z)
    log_keep = jnp.where(allowed, log_beta - z, 0.0)
    later = lax.cumsum(log_keep, axis=z.ndim - 1, reverse=True) - log_keep
    return jnp.where(allowed, jnp.exp(log_beta + later), 0.0)


def _sb_prompt(q, k, v):
    b, s, h, d = q.shape
    nq = s // Q_BLOCK
    kpos = jnp.arange(s)
    v32 = v.astype(F32)

    def block(args):
        qi, i = args
        qpos = i * Q_BLOCK + jnp.arange(Q_BLOCK)
        z = jnp.einsum("bqhd,bkhd->bhqk", qi, k, preferred_element_type=F32) * ATTN_SCALE
        a = _stick_breaking_weights(z, (kpos[None, :] < qpos[:, None])[None, None])
        return jnp.einsum("bhqk,bkhd->bqhd", a, v32).astype(q.dtype)

    q_blocks = q.reshape(b, nq, Q_BLOCK, h, d).swapaxes(0, 1)
    out = lax.map(block, (q_blocks, jnp.arange(nq)))
    return out.swapaxes(0, 1).reshape(b, s, h, d)


def _sb_sample(q, k_new, v_new, cache_k, cache_v, layer, page_table):
    db, t, h, d = q.shape
    past = page_table.shape[1] * PAGE_SIZE
    k_past = cache_k[layer, page_table].reshape(db, past, h, d)
    v_past = cache_v[layer, page_table].reshape(db, past, h, d)
    z = jnp.concatenate([
        jnp.einsum("bthd,bshd->bhts", q, k_past, preferred_element_type=F32),
        jnp.einsum("bthd,bshd->bhts", q, k_new, preferred_element_type=F32)], axis=-1) * ATTN_SCALE
    qpos = past + jnp.arange(t)
    kpos = jnp.arange(past + t)
    a = _stick_breaking_weights(z, (kpos[None, :] < qpos[:, None])[None, None])
    out = (jnp.einsum("bhts,bshd->bthd", a[..., :past], v_past.astype(F32))
           + jnp.einsum("bhts,bshd->bthd", a[..., past:], v_new.astype(F32)))
    return out.astype(q.dtype)


def setup_inputs(seed: int = 0) -> dict:
    key = jax.random.key(seed)
    ks = jax.random.split(key, 18)
    n_pages = PAST_LEN // PAGE_SIZE
    n_used = DEC_BATCH * n_pages
    n_pool = n_used + n_used // 4
    nrm = lambda k_, shp: jax.random.normal(k_, shp, dtype=F32)
    shp_a = (DEPTH, n_pool, PAGE_SIZE, N_HEADS_MOBA, HEAD_DIM)
    shp_b = (DEPTH, n_pool, PAGE_SIZE, N_HEADS_SB, HEAD_DIM)
    page_table = jax.random.permutation(ks[6], n_pool)[:n_used].reshape(DEC_BATCH, n_pages).astype(jnp.int32)
    return {
        "x_prompt": nrm(ks[0], (BATCH, SEQ, D_MODEL)),
        "x_sample": nrm(ks[1], (DEC_BATCH, DEC_SEQ, D_MODEL)),
        "cache_k_moba": nrm(ks[2], shp_a),
        "cache_v_moba": nrm(ks[3], shp_a),
        "cache_k_sb": nrm(ks[4], shp_b),
        "cache_v_sb": nrm(ks[5], shp_b),
        "page_table": page_table,
        "g_attn_norm": 1.0 + 0.02 * nrm(ks[7], (DEPTH, D_MODEL)),
        "w_in": nrm(ks[8], (DEPTH, D_MODEL, D_PROJ)) * D_MODEL ** -0.5,
        "g_q_norm": 1.0 + 0.02 * nrm(ks[9], (DEPTH, HEAD_DIM)),
        "g_k_norm": 1.0 + 0.02 * nrm(ks[10], (DEPTH, HEAD_DIM)),
        "g_out_moba": 1.0 + 0.02 * nrm(ks[11], (DEPTH, W_MOBA)),
        "g_out_sb": 1.0 + 0.02 * nrm(ks[12], (DEPTH, W_SB)),
        "w_out": nrm(ks[13], (DEPTH, D_MIX, D_MODEL)) * D_MIX ** -0.5,
        "g_mlp_norm": 1.0 + 0.02 * nrm(ks[14], (DEPTH, D_MODEL)),
        "w_up": nrm(ks[15], (DEPTH, D_MODEL, D_FF)) * D_MODEL ** -0.5,
        "w_down": nrm(ks[16], (DEPTH, D_FF, D_MODEL)) * D_FF ** -0.5,
    }


def reference(x_prompt, x_sample, cache_k_moba, cache_v_moba, cache_k_sb, cache_v_sb, page_table,
              g_attn_norm, w_in, g_q_norm, g_k_norm, g_out_moba, g_out_sb, w_out,
              g_mlp_norm, w_up, w_down):
    slopes = _alibi_slopes()
    yp, ys = x_prompt, x_sample
    kmp, vmp, ksp, vsp = [], [], [], []
    kms, vms, kss, vss = [], [], [], []
    for layer in range(DEPTH):
        qa, ka, va, qb, kb, vb = _project(yp, g_attn_norm[layer], w_in[layer], g_q_norm[layer], g_k_norm[layer])
        oa = _moba_prompt(qa, ka, va, slopes)
        ob = _sb_prompt(qb, kb, vb)
        yp = _merge_mlp(yp, oa, ob, g_out_moba[layer], g_out_sb[layer], w_out[layer],
                        g_mlp_norm[layer], w_up[layer], w_down[layer])
        kmp.append(ka); vmp.append(va); ksp.append(kb); vsp.append(vb)
        qa, ka, va, qb, kb, vb = _project(ys, g_attn_norm[layer], w_in[layer], g_q_norm[layer], g_k_norm[layer])
        oa = _moba_sample(qa, ka, va, cache_k_moba, cache_v_moba, layer, page_table, slopes)
        ob = _sb_sample(qb, kb, vb, cache_k_sb, cache_v_sb, layer, page_table)
        ys = _merge_mlp(ys, oa, ob, g_out_moba[layer], g_out_sb[layer], w_out[layer],
                        g_mlp_norm[layer], w_up[layer], w_down[layer])
        kms.append(ka); vms.append(va); kss.append(kb); vss.append(vb)
    return (yp, ys, jnp.stack(kmp), jnp.stack(vmp), jnp.stack(ksp), jnp.stack(vsp),
            jnp.stack(kms), jnp.stack(vms), jnp.stack(kss), jnp.stack(vss))
```

```python
import functools

import jax
import jax.numpy as jnp
from jax import lax
from jax.experimental import pallas as pl
from jax.experimental.pallas import tpu as pltpu

F32 = jnp.float32
BF16 = jnp.bfloat16

HEAD_DIM = 64
N_HEADS = 8
W_GROUP = N_HEADS * HEAD_DIM
LANES = 128
N_PAIRS = W_GROUP // LANES
MOBA_BLOCK = 256
MOBA_TOPK = 3
PAGE_SIZE = 128
EPS = 1e-6
ATTN_SCALE = HEAD_DIM ** -0.5
NEG = -1e30
SB_DEAD = -104.0
VMEM_LIMIT = 56 * 1024 * 1024

_NT = (((1,), (1,)), ((), ()))


def _cparams(sem):
    return pltpu.CompilerParams(dimension_semantics=sem, vmem_limit_bytes=VMEM_LIMIT)


def _split_dot(x, m):
    hi = x.astype(BF16)
    lo = (x - hi.astype(F32)).astype(BF16)
    return (jnp.dot(hi, m, preferred_element_type=F32)
            + jnp.dot(lo, m, preferred_element_type=F32))


def _log_sigmoid(z):
    return jnp.minimum(z, 0.0) - jnp.log1p(jnp.exp(-jnp.abs(z)))


def _proj_kernel(x_ref, g_ref, w_ref, gq_ref, gk_ref, hm_ref,
                 qa_ref, ka_ref, va_ref, kb_ref, vb_ref,
                 qb16_ref, ka16_ref, va16_ref, kb16_ref, vb16_ref, ksum_ref):
    x = x_ref[...]
    xn = x * lax.rsqrt(jnp.mean(x * x, axis=-1, keepdims=True) + EPS) * g_ref[...]
    xn16 = xn.astype(BF16)

    def slab(c):
        return jnp.dot(xn16, w_ref[:, c * W_GROUP:(c + 1) * W_GROUP], preferred_element_type=F32)

    def head_norm(h, g):
        ms = _split_dot(h * h, hm_ref[...]) * (1.0 / HEAD_DIM)
        return h * lax.rsqrt(ms + EPS) * g

    qa_ref[...] = head_norm(slab(0), gq_ref[...])
    ka = head_norm(slab(1), gk_ref[...])
    ka_ref[...] = ka
    ka16_ref[...] = ka.astype(BF16)
    tm = ka.shape[0]
    if tm % MOBA_BLOCK == 0:
        ksum_ref[0] = jnp.sum(ka.reshape(tm // MOBA_BLOCK, MOBA_BLOCK, W_GROUP), axis=1)
    else:
        ksum_ref[0] = jnp.zeros(ksum_ref.shape[1:], F32)
    va = slab(2)
    va_ref[...] = va
    va16_ref[...] = va.astype(BF16)
    qb16_ref[...] = (slab(3) * ATTN_SCALE).astype(BF16)
    kb = slab(4)
    kb_ref[...] = kb
    kb16_ref[...] = kb.astype(BF16)
    vb = slab(5)
    vb_ref[...] = vb
    vb16_ref[...] = vb.astype(BF16)


def _project(x2d, g_attn, w_in16, gq_t, gk_t, head_mat, tm):
    rows, d_model = x2d.shape
    n_tiles = rows // tm
    n_sum = max(tm // MOBA_BLOCK, 1)
    row_spec = lambda w: pl.BlockSpec((tm, w), lambda i: (i, 0))
    whole = pl.BlockSpec(memory_space=pltpu.VMEM)
    f32_out = jax.ShapeDtypeStruct((rows, W_GROUP), F32)
    b16_out = jax.ShapeDtypeStruct((rows, W_GROUP), BF16)
    return pl.pallas_call(
        _proj_kernel,
        grid=(n_tiles,),
        in_specs=[row_spec(d_model), whole, whole, whole, whole, whole],
        out_specs=[row_spec(W_GROUP)] * 10 + [pl.BlockSpec((1, n_sum, W_GROUP), lambda i: (i, 0, 0))],
        out_shape=[f32_out] * 5 + [b16_out] * 5 + [jax.ShapeDtypeStruct((n_tiles, n_sum, W_GROUP), F32)],
        compiler_params=_cparams(("parallel",)),
        name="project",
    )(x2d, g_attn, w_in16, gq_t, gk_t, head_mat)


def _sb_prompt_kernel(q_ref, k_ref, v_ref, u_ref, o_ref, acc_ref, carry_ref, *, tq):
    i = pl.program_id(2)
    q2 = q_ref[0]
    lane = lax.broadcasted_iota(jnp.int32, (1, LANES), 1)
    row = lax.broadcasted_iota(jnp.int32, (tq, tq), 0)
    col = lax.broadcasted_iota(jnp.int32, (tq, tq), 1)
    below_diag = col < row
    outs = []
    for hh in range(2):
        in_head = (lane >= hh * HEAD_DIM) & (lane < (hh + 1) * HEAD_DIM)
        qm = jnp.where(in_head, q2, jnp.zeros_like(q2))

        def block(j, diag):
            start = pl.multiple_of(j * tq, tq)
            kblk = k_ref[0, pl.ds(start, tq), :]
            vblk = v_ref[0, pl.ds(start, tq), :]
            z = lax.dot_general(qm, kblk, _NT, preferred_element_type=F32)
            log_beta = _log_sigmoid(z)
            log_keep = log_beta - z
            if diag:
                log_keep = jnp.where(below_diag, log_keep, 0.0)
            later = _split_dot(log_keep, u_ref[...]) + carry_ref[...]
            a = jnp.exp(log_beta + later)
            if diag:
                a = jnp.where(below_diag, a, 0.0)
            acc_ref[...] += jnp.dot(a.astype(BF16), vblk, preferred_element_type=F32)
            carry_ref[...] += jnp.sum(log_keep, axis=-1, keepdims=True)

        acc_ref[...] = jnp.zeros_like(acc_ref)
        carry_ref[...] = jnp.zeros_like(carry_ref)
        block(i, True)

        def cond(j):
            return jnp.logical_and(j >= 0, jnp.max(carry_ref[...]) > SB_DEAD)

        def body(j):
            block(j, False)
            return j - 1

        lax.while_loop(cond, body, i - 1)
        outs.append(acc_ref[...])
    o_ref[0] = jnp.where(lane < HEAD_DIM, outs[0], outs[1])


def _sb_prompt(q16, k16, v16, upper, tq):
    b, s, _ = q16.shape
    grid = (b, N_PAIRS, s // tq)
    q_spec = pl.BlockSpec((1, tq, LANES), lambda bi, pr, i: (bi, i, pr))
    kv_spec = pl.BlockSpec((1, s, LANES), lambda bi, pr, i: (bi, 0, pr))
    return pl.pallas_call(
        functools.partial(_sb_prompt_kernel, tq=tq),
        grid=grid,
        in_specs=[q_spec, kv_spec, kv_spec, pl.BlockSpec(memory_space=pltpu.VMEM)],
        out_specs=q_spec,
        out_shape=jax.ShapeDtypeStruct((b, s, W_GROUP), F32),
        scratch_shapes=[pltpu.VMEM((tq, LANES), F32), pltpu.VMEM((tq, 1), F32)],
        compiler_params=_cparams(("parallel", "parallel", "arbitrary")),
        name="sb_prompt",
    )(q16, k16, v16, upper)


def _top_k_bias(scores, cand_mask, idx):
    n_idx = scores.shape[-1]
    cand = jnp.where(cand_mask, scores, -jnp.inf)
    bias = jnp.full(scores.shape, NEG, F32)
    for _ in range(MOBA_TOPK):
        best = jnp.max(cand, axis=-1, keepdims=True)
        is_best = (cand == best) & (best > -jnp.inf)
        first = jnp.min(jnp.where(is_best, idx, n_idx), axis=-1, keepdims=True)
        pick = idx == first
        bias = jnp.where(pick, 0.0, bias)
        cand = jnp.where(pick, -jnp.inf, cand)
    return bias


def _moba_prompt_kernel(slopes_ref, q_ref, k_ref, v_ref, kmean_ref, o_ref,
                        acc_ref, m_ref, l_ref, bias_ref):
    pr = pl.program_id(1)
    i = pl.program_id(2)
    tq = MOBA_BLOCK
    n_blocks = kmean_ref.shape[1]
    q2 = q_ref[0]
    lane = lax.broadcasted_iota(jnp.int32, (1, LANES), 1)
    row = lax.broadcasted_iota(jnp.int32, (tq, tq), 0)
    col = lax.broadcasted_iota(jnp.int32, (tq, tq), 1)
    rel = (col - row).astype(F32)
    blk_idx = lax.broadcasted_iota(jnp.int32, (tq, n_blocks), 1)
    outs = []
    for hh in range(2):
        slope = slopes_ref[2 * pr + hh]
        in_head = (lane >= hh * HEAD_DIM) & (lane < (hh + 1) * HEAD_DIM)
        qf = jnp.where(in_head, q2, 0.0)
        scores = lax.dot_general(qf, kmean_ref[0], _NT, preferred_element_type=F32,
                                 precision=lax.Precision.HIGHEST)
        bias_ref[...] = _top_k_bias(scores, blk_idx < i, blk_idx)
        q16 = (qf * ATTN_SCALE).astype(BF16)

        def logits(n):
            start = pl.multiple_of(n * tq, tq)
            kblk = k_ref[0, pl.ds(start, tq), :]
            s = lax.dot_general(q16, kblk, _NT, preferred_element_type=F32)
            dist = rel + ((n - i) * tq).astype(F32)
            return s + slope * dist, v_ref[0, pl.ds(start, tq), :]

        z, vblk = logits(i)
        z = jnp.where(col <= row, z, NEG)
        m0 = jnp.max(z, axis=-1, keepdims=True)
        p = jnp.exp(z - m0)
        m_ref[...] = m0
        l_ref[...] = jnp.sum(p, axis=-1, keepdims=True)
        acc_ref[...] = jnp.dot(p.astype(BF16), vblk, preferred_element_type=F32)

        def body(n, _):
            z, vblk = logits(n)
            sel = jnp.sum(jnp.where(blk_idx == n, bias_ref[...], 0.0), axis=-1, keepdims=True)
            z = z + sel
            m_old = m_ref[...]
            m_new = jnp.maximum(m_old, jnp.max(z, axis=-1, keepdims=True))
            alpha = jnp.exp(m_old - m_new)
            p = jnp.exp(z - m_new)
            m_ref[...] = m_new
            l_ref[...] = alpha * l_ref[...] + jnp.sum(p, axis=-1, keepdims=True)
            acc_ref[...] = alpha * acc_ref[...] + jnp.dot(p.astype(BF16), vblk,
                                                          preferred_element_type=F32)
            return 0

        lax.fori_loop(0, i, body, 0)
        outs.append(acc_ref[...] / l_ref[...])
    o_ref[0] = jnp.where(lane < HEAD_DIM, outs[0], outs[1])


def _moba_prompt(slopes, qa, k16, v16, kmean):
    b, s, _ = qa.shape
    tq = MOBA_BLOCK
    n_blocks = kmean.shape[1]
    grid = (b, N_PAIRS, s // tq)
    q_spec = pl.BlockSpec((1, tq, LANES), lambda bi, pr, i: (bi, i, pr))
    kv_spec = pl.BlockSpec((1, s, LANES), lambda bi, pr, i: (bi, 0, pr))
    return pl.pallas_call(
        _moba_prompt_kernel,
        grid=grid,
        in_specs=[pl.BlockSpec(memory_space=pltpu.SMEM), q_spec, kv_spec, kv_spec,
                  pl.BlockSpec((1, n_blocks, LANES), lambda bi, pr, i: (bi, 0, pr))],
        out_specs=q_spec,
        out_shape=jax.ShapeDtypeStruct((b, s, W_GROUP), F32),
        scratch_shapes=[pltpu.VMEM((tq, LANES), F32), pltpu.VMEM((tq, 1), F32),
                        pltpu.VMEM((tq, 1), F32), pltpu.VMEM((tq, n_blocks), F32)],
        compiler_params=_cparams(("parallel", "parallel", "arbitrary")),
        name="moba_prompt",
    )(slopes, qa, k16, v16, kmean)


def _merge_mlp_kernel(x_ref, oa_ref, ob_ref, goa_ref, gob_ref, wout_ref, gmlp_ref, wup_ref,
                      wdown_ref, y_ref, *, ff_chunk):
    def norm(v, g):
        return v * lax.rsqrt(jnp.mean(v * v, axis=-1, keepdims=True) + EPS) * g

    na = norm(oa_ref[...], goa_ref[...]).astype(BF16)
    nb = norm(ob_ref[...], gob_ref[...]).astype(BF16)
    mixed = jnp.concatenate([na, nb], axis=-1)
    h = x_ref[...] + jnp.dot(mixed, wout_ref[...], preferred_element_type=F32)
    hn16 = norm(h, gmlp_ref[...]).astype(BF16)
    y = h
    d_ff = wup_ref.shape[1]
    for c in range(d_ff // ff_chunk):
        u = jnp.dot(hn16, wup_ref[:, c * ff_chunk:(c + 1) * ff_chunk], preferred_element_type=F32)
        act = jnp.square(jnp.maximum(u, 0.0)).astype(BF16)
        y = y + jnp.dot(act, wdown_ref[c * ff_chunk:(c + 1) * ff_chunk, :],
                        preferred_element_type=F32)
    y_ref[...] = y


def _merge_mlp(x2d, oa, ob, g_oa, g_ob, w_out16, g_mlp, w_up16, w_down16, tm):
    rows, d_model = x2d.shape
    row_spec = lambda w: pl.BlockSpec((tm, w), lambda i: (i, 0))
    whole = pl.BlockSpec(memory_space=pltpu.VMEM)
    return pl.pallas_call(
        functools.partial(_merge_mlp_kernel, ff_chunk=1024),
        grid=(rows // tm,),
        in_specs=[row_spec(d_model), row_spec(W_GROUP), row_spec(W_GROUP),
                  whole, whole, whole, whole, whole, whole],
        out_specs=row_spec(d_model),
        out_shape=jax.ShapeDtypeStruct((rows, d_model), F32),
        compiler_params=_cparams(("parallel",)),
        name="merge_mlp",
    )(x2d, oa, ob, g_oa, g_ob, w_out16, g_mlp, w_up16, w_down16)


def _page_sum_kernel(pt_ref, k_ref, o_ref):
    p = pl.program_id(1)
    s = jnp.sum(k_ref[...], axis=0)

    @pl.when(p % 2 == 0)
    def _():
        o_ref[...] = s

    @pl.when(p % 2 == 1)
    def _():
        o_ref[...] += s


def _page_sums(page_table, cache_k):
    db, n_pages = page_table.shape
    pages_per_block = MOBA_BLOCK // PAGE_SIZE
    page_spec = pl.BlockSpec((None, None, PAGE_SIZE, N_HEADS, HEAD_DIM),
                             lambda b, p, pt: (0, pt[b, p], 0, 0, 0))
    return pl.pallas_call(
        _page_sum_kernel,
        grid_spec=pltpu.PrefetchScalarGridSpec(
            num_scalar_prefetch=1,
            grid=(db, n_pages),
            in_specs=[page_spec],
            out_specs=pl.BlockSpec((None, None, N_HEADS, HEAD_DIM),
                                   lambda b, p, pt: (b, p // pages_per_block, 0, 0)),
        ),
        out_shape=jax.ShapeDtypeStruct((db, n_pages // pages_per_block, N_HEADS, HEAD_DIM), F32),
        compiler_params=_cparams(("parallel", "arbitrary")),
        name="page_sums",
    )(page_table, cache_k)


def _sample_attn_kernel(pt_ref, slopes_ref, qa_ref, qb_ref, kan_ref, van_ref, ksum_ref,
                        ka_ref, va_ref, kb_ref, vb_ref, oa_ref, ob_ref,
                        bias_ref, m_ref, l_ref, acca_ref, carry_ref, accb_ref, *, n_pages):
    step = pl.program_id(1)
    page = n_pages - 1 - step
    past = n_pages * PAGE_SIZE
    n_blocks = ksum_ref.shape[0]
    rows_pp = PAGE_SIZE * N_HEADS
    rows_pb = n_blocks * N_HEADS
    sub = lax.broadcasted_iota(jnp.int32, (N_HEADS, rows_pp), 0)
    ln = lax.broadcasted_iota(jnp.int32, (N_HEADS, rows_pp), 1)
    diag = (ln % N_HEADS) == sub
    slot = ln // N_HEADS
    slopes = slopes_ref[...]
    qa = qa_ref[...]

    @pl.when(step == 0)
    def _init():
        ks = ksum_ref[...].reshape(rows_pb, HEAD_DIM)
        sc = lax.dot_general(qa, ks, _NT, preferred_element_type=F32,
                             precision=lax.Precision.HIGHEST)
        sub_b = lax.broadcasted_iota(jnp.int32, (N_HEADS, rows_pb), 0)
        ln_b = lax.broadcasted_iota(jnp.int32, (N_HEADS, rows_pb), 1)
        diag_b = (ln_b % N_HEADS) == sub_b
        bias = _top_k_bias(sc, diag_b, ln_b // N_HEADS)
        bias_ref[...] = jnp.where(diag_b, bias, 0.0)
        z_new = jnp.sum(qa * kan_ref[...], axis=-1, keepdims=True) * ATTN_SCALE
        m_ref[...] = z_new
        l_ref[...] = jnp.ones_like(l_ref)
        acca_ref[...] = van_ref[...]
        carry_ref[...] = jnp.zeros_like(carry_ref)
        accb_ref[...] = jnp.zeros_like(accb_ref)

    blk = page // (MOBA_BLOCK // PAGE_SIZE)
    ln_b = lax.broadcasted_iota(jnp.int32, (N_HEADS, rows_pb), 1)
    sel = jnp.sum(jnp.where(ln_b // N_HEADS == blk, bias_ref[...], 0.0), axis=-1, keepdims=True)
    k_rows = ka_ref[...].reshape(rows_pp, HEAD_DIM).astype(BF16)
    v_rows = va_ref[...].reshape(rows_pp, HEAD_DIM).astype(BF16)
    q16 = (qa * ATTN_SCALE).astype(BF16)
    s = lax.dot_general(q16, k_rows, _NT, preferred_element_type=F32)
    dist = (page * PAGE_SIZE + slot - past).astype(F32)
    z = jnp.where(diag, s + slopes * dist + sel, NEG)
    m_old = m_ref[...]
    m_new = jnp.maximum(m_old, jnp.max(z, axis=-1, keepdims=True))
    alpha = jnp.exp(m_old - m_new)
    p = jnp.where(diag, jnp.exp(z - m_new), 0.0)
    m_ref[...] = m_new
    l_ref[...] = alpha * l_ref[...] + jnp.sum(p, axis=-1, keepdims=True)
    acca_ref[...] = alpha * acca_ref[...] + jnp.dot(p.astype(BF16), v_rows,
                                                    preferred_element_type=F32)

    @pl.when(jnp.max(carry_ref[...]) > SB_DEAD)
    def _sb():
        kb_rows = kb_ref[...].reshape(rows_pp, HEAD_DIM).astype(BF16)
        vb_rows = vb_ref[...].reshape(rows_pp, HEAD_DIM).astype(BF16)
        zb = lax.dot_general(qb_ref[...], kb_rows, _NT, preferred_element_type=F32)
        log_beta = _log_sigmoid(zb)
        log_keep = jnp.where(diag, log_beta - zb, 0.0)
        suffix = log_keep
        shift = N_HEADS
        while shift < rows_pp:
            rolled = pltpu.roll(suffix, rows_pp - shift, 1)
            suffix = suffix + jnp.where(ln + shift < rows_pp, rolled, 0.0)
            shift *= 2
        later = suffix - log_keep + carry_ref[...]
        a = jnp.where(diag, jnp.exp(log_beta + later), 0.0)
        accb_ref[...] += jnp.dot(a.astype(BF16), vb_rows, preferred_element_type=F32)
        carry_ref[...] += jnp.sum(log_keep, axis=-1, keepdims=True)

    @pl.when(step == n_pages - 1)
    def _fin():
        oa_ref[...] = acca_ref[...] / l_ref[...]
        ob_ref[...] = accb_ref[...]


def _sample_attn(page_table, slopes_col, qa, qb16, ka_new, va_new, ksum, ck_a, cv_a, ck_b, cv_b):
    db, n_pages = page_table.shape
    n_blocks = ksum.shape[1]
    page_spec = pl.BlockSpec((None, None, PAGE_SIZE, N_HEADS, HEAD_DIM),
                             lambda b, p, pt: (0, pt[b, n_pages - 1 - p], 0, 0, 0))
    tok_spec = pl.BlockSpec((None, N_HEADS, HEAD_DIM), lambda b, p, pt: (b, 0, 0))
    whole = pl.BlockSpec(memory_space=pltpu.VMEM)
    tok_out = jax.ShapeDtypeStruct((db, N_HEADS, HEAD_DIM), F32)
    return pl.pallas_call(
        functools.partial(_sample_attn_kernel, n_pages=n_pages),
        grid_spec=pltpu.PrefetchScalarGridSpec(
            num_scalar_prefetch=1,
            grid=(db, n_pages),
            in_specs=[whole, tok_spec, tok_spec, tok_spec, tok_spec,
                      pl.BlockSpec((None, n_blocks, N_HEADS, HEAD_DIM),
                                   lambda b, p, pt: (b, 0, 0, 0)),
                      page_spec, page_spec, page_spec, page_spec],
            out_specs=[tok_spec, tok_spec],
            scratch_shapes=[pltpu.VMEM((N_HEADS, n_blocks * N_HEADS), F32),
                            pltpu.VMEM((N_HEADS, 1), F32), pltpu.VMEM((N_HEADS, 1), F32),
                            pltpu.VMEM((N_HEADS, HEAD_DIM), F32),
                            pltpu.VMEM((N_HEADS, 1), F32),
                            pltpu.VMEM((N_HEADS, HEAD_DIM), F32)],
        ),
        out_shape=[tok_out, tok_out],
        compiler_params=_cparams(("parallel", "arbitrary")),
        name="sample_attn",
    )(page_table, slopes_col, qa, qb16, ka_new, va_new, ksum, ck_a, cv_a, ck_b, cv_b)


def _row_tile(rows, cap):
    tm = min(rows, cap)
    assert rows % tm == 0, (rows, tm)
    return tm


def kernel(x_prompt, x_sample, cache_k_moba, cache_v_moba, cache_k_sb, cache_v_sb, page_table,
           g_attn_norm, w_in, g_q_norm, g_k_norm, g_out_moba, g_out_sb, w_out,
           g_mlp_norm, w_up, w_down):
    depth = w_in.shape[0]
    assert depth == 1, "the paged caches are read with a static layer index"
    b, s, d_model = x_prompt.shape
    db, t, _ = x_sample.shape
    assert t == 1 and s % MOBA_BLOCK == 0
    n_pages = page_table.shape[1]
    assert (n_pages * PAGE_SIZE) % MOBA_BLOCK == 0

    slopes = jnp.exp2(-8.0 * jnp.arange(1, N_HEADS + 1, dtype=F32) / N_HEADS)
    hd = jnp.arange(W_GROUP) // HEAD_DIM
    head_mat = (hd[:, None] == hd[None, :]).astype(BF16)
    tq = MOBA_BLOCK
    ki = jnp.arange(tq)
    upper = (ki[:, None] > ki[None, :]).astype(BF16)

    layer = 0
    row = lambda g: g[layer].reshape(1, -1)
    gq_t = jnp.tile(g_q_norm[layer], N_HEADS).reshape(1, W_GROUP)
    gk_t = jnp.tile(g_k_norm[layer], N_HEADS).reshape(1, W_GROUP)
    w_in16 = w_in[layer].astype(BF16)
    w_out16 = w_out[layer].astype(BF16)
    w_up16 = w_up[layer].astype(BF16)
    w_down16 = w_down[layer].astype(BF16)

    xp = x_prompt.reshape(b * s, d_model)
    tm = _row_tile(b * s, 512)
    (qa, ka, va, kb, vb, qb16, ka16, va16, kb16, vb16, ksum) = _project(
        xp, row(g_attn_norm), w_in16, gq_t, gk_t, head_mat, tm)
    kmean = ksum.reshape(b, s // MOBA_BLOCK, W_GROUP) * (1.0 / MOBA_BLOCK)
    r3 = lambda a: a.reshape(b, s, W_GROUP)
    oa = _moba_prompt(slopes, r3(qa), r3(ka16), r3(va16), kmean)
    ob = _sb_prompt(r3(qb16), r3(kb16), r3(vb16), upper, tq)
    yp = _merge_mlp(xp, oa.reshape(b * s, W_GROUP), ob.reshape(b * s, W_GROUP),
                    row(g_out_moba), row(g_out_sb), w_out16, row(g_mlp_norm), w_up16, w_down16, tm)

    xs = x_sample.reshape(db, d_model)
    tms = _row_tile(db, 512)
    (qa_s, ka_s, va_s, kb_s, vb_s, qb16_s, _, _, _, _, _) = _project(
        xs, row(g_attn_norm), w_in16, gq_t, gk_t, head_mat, tms)
    tok = lambda a: a.reshape(db, N_HEADS, HEAD_DIM)
    ksum_s = _page_sums(page_table, cache_k_moba)
    oa_s, ob_s = _sample_attn(page_table, slopes.reshape(N_HEADS, 1), tok(qa_s), tok(qb16_s),
                              tok(ka_s), tok(va_s), ksum_s,
                              cache_k_moba, cache_v_moba, cache_k_sb, cache_v_sb)
    ys = _merge_mlp(xs, oa_s.reshape(db, W_GROUP), ob_s.reshape(db, W_GROUP),
                    row(g_out_moba), row(g_out_sb), w_out16, row(g_mlp_norm), w_up16, w_down16, tms)

    p5 = lambda a: a.reshape(depth, b, s, N_HEADS, HEAD_DIM)
    s5 = lambda a: a.reshape(depth, db, t, N_HEADS, HEAD_DIM)
    return (yp.reshape(b, s, d_model), ys.reshape(db, t, d_model),
            p5(ka), p5(va), p5(kb), p5(vb), s5(ka_s), s5(va_s), s5(kb_s), s5(vb_s))
```

```python
import functools

import jax
import jax.numpy as jnp
from jax import lax
from jax.experimental import pallas as pl
from jax.experimental.pallas import tpu as pltpu

F32 = jnp.float32
BF16 = jnp.bfloat16

HEAD_DIM = 64
N_HEADS = 8
W_GROUP = N_HEADS * HEAD_DIM
LANES = 128
N_PAIRS = W_GROUP // LANES
MOBA_BLOCK = 256
MOBA_TOPK = 3
PAGE_SIZE = 128
EPS = 1e-6
ATTN_SCALE = HEAD_DIM ** -0.5
NEG = -1e30
SB_DEAD = -104.0
VMEM_LIMIT = 56 * 1024 * 1024

_NT = (((1,), (1,)), ((), ()))


def _cparams(sem):
    return pltpu.CompilerParams(dimension_semantics=sem, vmem_limit_bytes=VMEM_LIMIT)


def _split_dot(x, m):
    hi = x.astype(BF16)
    lo = (x - hi.astype(F32)).astype(BF16)
    return (jnp.dot(hi, m, preferred_element_type=F32)
            + jnp.dot(lo, m, preferred_element_type=F32))


def _log_sigmoid(z):
    return jnp.minimum(z, 0.0) - jnp.log1p(jnp.exp(-jnp.abs(z)))


def _proj_kernel(x_ref, g_ref, w_ref, gq_ref, gk_ref, hm_ref,
                 ka_ref, va_ref, kb_ref, vb_ref, qb16_ref, ka16_ref, kb16_ref, vb16_ref, *extra,
                 transposed):
    x = x_ref[...]
    xn = x * lax.rsqrt(jnp.mean(x * x, axis=-1, keepdims=True) + EPS) * g_ref[...]
    xn16 = xn.astype(BF16)

    def slab(c):
        return jnp.dot(xn16, w_ref[:, c * W_GROUP:(c + 1) * W_GROUP], preferred_element_type=F32)

    def head_norm(h, g):
        ms = _split_dot(h * h, hm_ref[...]) * (1.0 / HEAD_DIM)
        return h * lax.rsqrt(ms + EPS) * g

    qa = head_norm(slab(0), gq_ref[...])
    ka = head_norm(slab(1), gk_ref[...])
    va = slab(2)
    ka_ref[...] = ka
    ka16_ref[...] = ka.astype(BF16)
    va_ref[...] = va
    if transposed:
        qat_ref, vat_ref, ksum_ref = extra
        n_sub = ka.shape[0] // MOBA_BLOCK
        qa_t = qa.T
        va_t = va.T.astype(BF16)
        for j in range(n_sub):
            qat_ref[j] = qa_t[:, j * MOBA_BLOCK:(j + 1) * MOBA_BLOCK]
            vat_ref[j] = va_t[:, j * MOBA_BLOCK:(j + 1) * MOBA_BLOCK]
        ksum_ref[0] = jnp.sum(ka.reshape(n_sub, MOBA_BLOCK, W_GROUP), axis=1)
    else:
        (qa_ref,) = extra
        qa_ref[...] = qa
    qb16_ref[...] = (slab(3) * ATTN_SCALE).astype(BF16)
    kb = slab(4)
    kb_ref[...] = kb
    kb16_ref[...] = kb.astype(BF16)
    vb = slab(5)
    vb_ref[...] = vb
    vb16_ref[...] = vb.astype(BF16)


def _project(x2d, g_attn, w_in16, gq_t, gk_t, head_mat, tm, transposed):
    rows, d_model = x2d.shape
    n_tiles = rows // tm
    row_spec = lambda w: pl.BlockSpec((tm, w), lambda i: (i, 0))
    whole = pl.BlockSpec(memory_space=pltpu.VMEM)
    f32_out = jax.ShapeDtypeStruct((rows, W_GROUP), F32)
    b16_out = jax.ShapeDtypeStruct((rows, W_GROUP), BF16)
    out_specs = [row_spec(W_GROUP)] * 8
    out_shape = [f32_out] * 4 + [b16_out] * 4
    if transposed:
        assert tm % MOBA_BLOCK == 0
        n_sub = tm // MOBA_BLOCK
        t_spec = pl.BlockSpec((n_sub, W_GROUP, MOBA_BLOCK), lambda i: (i, 0, 0))
        out_specs += [t_spec, t_spec, pl.BlockSpec((1, n_sub, W_GROUP), lambda i: (i, 0, 0))]
        out_shape += [jax.ShapeDtypeStruct((rows // MOBA_BLOCK, W_GROUP, MOBA_BLOCK), F32),
                      jax.ShapeDtypeStruct((rows // MOBA_BLOCK, W_GROUP, MOBA_BLOCK), BF16),
                      jax.ShapeDtypeStruct((n_tiles, n_sub, W_GROUP), F32)]
    else:
        out_specs += [row_spec(W_GROUP)]
        out_shape += [f32_out]
    return pl.pallas_call(
        functools.partial(_proj_kernel, transposed=transposed),
        grid=(n_tiles,),
        in_specs=[row_spec(d_model), whole, whole, whole, whole, whole],
        out_specs=out_specs,
        out_shape=out_shape,
        compiler_params=_cparams(("parallel",)),
        name="project",
    )(x2d, g_attn, w_in16, gq_t, gk_t, head_mat)


def _sb_prompt_kernel(q_ref, k_ref, v_ref, u_ref, o_ref, acc_ref, carry_ref, *, tq):
    i = pl.program_id(2)
    q2 = q_ref[0]
    lane = lax.broadcasted_iota(jnp.int32, (1, LANES), 1)
    row = lax.broadcasted_iota(jnp.int32, (tq, tq), 0)
    col = lax.broadcasted_iota(jnp.int32, (tq, tq), 1)
    below_diag = col < row
    outs = []
    for hh in range(2):
        in_head = (lane >= hh * HEAD_DIM) & (lane < (hh + 1) * HEAD_DIM)
        qm = jnp.where(in_head, q2, jnp.zeros_like(q2))

        def block(j, diag):
            start = pl.multiple_of(j * tq, tq)
            kblk = k_ref[0, pl.ds(start, tq), :]
            vblk = v_ref[0, pl.ds(start, tq), :]
            z = lax.dot_general(qm, kblk, _NT, preferred_element_type=F32)
            log_beta = _log_sigmoid(z)
            log_keep = log_beta - z
            if diag:
                log_keep = jnp.where(below_diag, log_keep, 0.0)
            later = _split_dot(log_keep, u_ref[...]) + carry_ref[...]
            a = jnp.exp(log_beta + later)
            if diag:
                a = jnp.where(below_diag, a, 0.0)
            acc_ref[...] += jnp.dot(a.astype(BF16), vblk, preferred_element_type=F32)
            carry_ref[...] += jnp.sum(log_keep, axis=-1, keepdims=True)

        acc_ref[...] = jnp.zeros_like(acc_ref)
        carry_ref[...] = jnp.zeros_like(carry_ref)
        block(i, True)

        def cond(j):
            return jnp.logical_and(j >= 0, jnp.max(carry_ref[...]) > SB_DEAD)

        def body(j):
            block(j, False)
            return j - 1

        lax.while_loop(cond, body, i - 1)
        outs.append(acc_ref[...])
    o_ref[0] = jnp.where(lane < HEAD_DIM, outs[0], outs[1])


def _sb_prompt(q16, k16, v16, upper, tq):
    b, s, _ = q16.shape
    grid = (b, N_PAIRS, s // tq)
    q_spec = pl.BlockSpec((1, tq, LANES), lambda bi, pr, i: (bi, i, pr))
    kv_spec = pl.BlockSpec((1, s, LANES), lambda bi, pr, i: (bi, 0, pr))
    return pl.pallas_call(
        functools.partial(_sb_prompt_kernel, tq=tq),
        grid=grid,
        in_specs=[q_spec, kv_spec, kv_spec, pl.BlockSpec(memory_space=pltpu.VMEM)],
        out_specs=q_spec,
        out_shape=jax.ShapeDtypeStruct((b, s, W_GROUP), F32),
        scratch_shapes=[pltpu.VMEM((tq, LANES), F32), pltpu.VMEM((tq, 1), F32)],
        compiler_params=_cparams(("parallel", "parallel", "arbitrary")),
        name="sb_prompt",
    )(q16, k16, v16, upper)


def _top_k_bias(scores, cand_mask, idx, axis):
    n_idx = scores.shape[axis]
    cand = jnp.where(cand_mask, scores, -jnp.inf)
    bias = jnp.full(scores.shape, NEG, F32)
    for _ in range(MOBA_TOPK):
        best = jnp.max(cand, axis=axis, keepdims=True)
        is_best = (cand == best) & (best > -jnp.inf)
        first = jnp.min(jnp.where(is_best, idx, n_idx), axis=axis, keepdims=True)
        pick = idx == first
        bias = jnp.where(pick, 0.0, bias)
        cand = jnp.where(pick, -jnp.inf, cand)
    return bias


def _moba_prompt_kernel(slopes_ref, qt_ref, k_ref, vt_ref, kmean_ref, o_ref,
                        acc_ref, m_ref, l_ref, bias_ref, alibi_ref, q16_ref):
    pr = pl.program_id(1)
    i = pl.program_id(2)
    tq = tk = MOBA_BLOCK
    n_blocks = kmean_ref.shape[1]
    qt2 = qt_ref[...]
    sub = lax.broadcasted_iota(jnp.int32, (LANES, 1), 0)
    key = lax.broadcasted_iota(jnp.int32, (tk, tq), 0)
    qry = lax.broadcasted_iota(jnp.int32, (tk, tq), 1)
    rel = (key - qry).astype(F32)
    blk_idx = lax.broadcasted_iota(jnp.int32, (n_blocks, tq), 0)
    own_start = pl.multiple_of(i * tk, tk)
    k_own = k_ref[0, pl.ds(own_start, tk), :]
    for hh in range(2):
        in_head = (sub >= hh * HEAD_DIM) & (sub < (hh + 1) * HEAD_DIM)
        qf = jnp.where(in_head, qt2, 0.0)
        scores = jnp.dot(kmean_ref[0], qf, preferred_element_type=F32,
                         precision=lax.Precision.HIGHEST)
        bias_ref[hh] = _top_k_bias(scores, blk_idx < i, blk_idx, 0)
        alibi = slopes_ref[2 * pr + hh] * rel
        alibi_ref[hh] = alibi
        q16 = (qf * ATTN_SCALE).astype(BF16)
        q16_ref[hh] = q16
        z = jnp.dot(k_own, q16, preferred_element_type=F32) + alibi
        z = jnp.where(key <= qry, z, NEG)
        m0 = jnp.max(z, axis=0, keepdims=True)
        p = jnp.exp(z - m0)
        m_ref[hh] = m0
        l_ref[hh] = jnp.sum(p, axis=0, keepdims=True)
        v_t = vt_ref[i, hh * HEAD_DIM:(hh + 1) * HEAD_DIM, :]
        acc_ref[hh] = jnp.dot(v_t, p.astype(BF16), preferred_element_type=F32)

    def body(n, _):
        kblk = k_ref[0, pl.ds(pl.multiple_of(n * tk, tk), tk), :]
        off = ((n - i) * tk).astype(F32)
        for hh in range(2):
            row = bias_ref[hh, pl.ds(n, 1), :] + slopes_ref[2 * pr + hh] * off
            z = jnp.dot(kblk, q16_ref[hh], preferred_element_type=F32) + alibi_ref[hh] + row
            m_old = m_ref[hh]
            m_new = jnp.maximum(m_old, jnp.max(z, axis=0, keepdims=True))
            alpha = jnp.exp(m_old - m_new)
            p = jnp.exp(z - m_new)
            m_ref[hh] = m_new
            l_ref[hh] = alpha * l_ref[hh] + jnp.sum(p, axis=0, keepdims=True)
            v_t = vt_ref[n, hh * HEAD_DIM:(hh + 1) * HEAD_DIM, :]
            acc_ref[hh] = alpha * acc_ref[hh] + jnp.dot(v_t, p.astype(BF16),
                                                        preferred_element_type=F32)
        return 0

    lax.fori_loop(0, i, body, 0)
    out_t = jnp.concatenate([acc_ref[0] / l_ref[0], acc_ref[1] / l_ref[1]], axis=0)
    o_ref[0] = out_t.T


def _moba_prompt(slopes, qa_t, k16, va_t, kmean):
    b, s, _ = k16.shape
    tq = MOBA_BLOCK
    n_blocks = s // tq
    grid = (b, N_PAIRS, n_blocks)
    return pl.pallas_call(
        _moba_prompt_kernel,
        grid=grid,
        in_specs=[pl.BlockSpec(memory_space=pltpu.SMEM),
                  pl.BlockSpec((None, LANES, tq), lambda bi, pr, i: (bi * n_blocks + i, pr, 0)),
                  pl.BlockSpec((1, s, LANES), lambda bi, pr, i: (bi, 0, pr)),
                  pl.BlockSpec((n_blocks, LANES, tq), lambda bi, pr, i: (bi, pr, 0)),
                  pl.BlockSpec((1, n_blocks, LANES), lambda bi, pr, i: (bi, 0, pr))],
        out_specs=pl.BlockSpec((1, tq, LANES), lambda bi, pr, i: (bi, i, pr)),
        out_shape=jax.ShapeDtypeStruct((b, s, W_GROUP), F32),
        scratch_shapes=[pltpu.VMEM((2, HEAD_DIM, tq), F32), pltpu.VMEM((2, 1, tq), F32),
                        pltpu.VMEM((2, 1, tq), F32), pltpu.VMEM((2, n_blocks, tq), F32),
                        pltpu.VMEM((2, tq, tq), F32), pltpu.VMEM((2, LANES, tq), BF16)],
        compiler_params=_cparams(("parallel", "parallel", "arbitrary")),
        name="moba_prompt",
    )(slopes, qa_t, k16, va_t, kmean)


def _merge_mlp_kernel(x_ref, oa_ref, ob_ref, goa_ref, gob_ref, wout_ref, gmlp_ref, wup_ref,
                      wdown_ref, y_ref, *, ff_chunk):
    def norm(v, g):
        return v * lax.rsqrt(jnp.mean(v * v, axis=-1, keepdims=True) + EPS) * g

    na = norm(oa_ref[...], goa_ref[...]).astype(BF16)
    nb = norm(ob_ref[...], gob_ref[...]).astype(BF16)
    mixed = jnp.concatenate([na, nb], axis=-1)
    h = x_ref[...] + jnp.dot(mixed, wout_ref[...], preferred_element_type=F32)
    hn16 = norm(h, gmlp_ref[...]).astype(BF16)
    y = h
    d_ff = wup_ref.shape[1]
    for c in range(d_ff // ff_chunk):
        u = jnp.dot(hn16, wup_ref[:, c * ff_chunk:(c + 1) * ff_chunk], preferred_element_type=F32)
        act = jnp.square(jnp.maximum(u, 0.0)).astype(BF16)
        y = y + jnp.dot(act, wdown_ref[c * ff_chunk:(c + 1) * ff_chunk, :],
                        preferred_element_type=F32)
    y_ref[...] = y


def _merge_mlp(x2d, oa, ob, g_oa, g_ob, w_out16, g_mlp, w_up16, w_down16, tm):
    rows, d_model = x2d.shape
    row_spec = lambda w: pl.BlockSpec((tm, w), lambda i: (i, 0))
    whole = pl.BlockSpec(memory_space=pltpu.VMEM)
    return pl.pallas_call(
        functools.partial(_merge_mlp_kernel, ff_chunk=1024),
        grid=(rows // tm,),
        in_specs=[row_spec(d_model), row_spec(W_GROUP), row_spec(W_GROUP),
                  whole, whole, whole, whole, whole, whole],
        out_specs=row_spec(d_model),
        out_shape=jax.ShapeDtypeStruct((rows, d_model), F32),
        compiler_params=_cparams(("parallel",)),
        name="merge_mlp",
    )(x2d, oa, ob, g_oa, g_ob, w_out16, g_mlp, w_up16, w_down16)


SCORE_PAGES = 16
SB_NEWEST_PAGES = 4


def _head_logits(q_bc, k_t):
    sub = lax.broadcasted_iota(jnp.int32, (N_HEADS, LANES), 0)
    z = jnp.zeros((N_HEADS, LANES), F32)
    for h in range(N_HEADS):
        z_h = jnp.sum(q_bc[h] * k_t[h], axis=0, keepdims=True)
        z = jnp.where(sub == h, z_h, z)
    return z


def _sample_scores_kernel(pt_ref, q_ref, *refs):
    page_refs, z_ref = refs[:-1], refs[-1]
    q_bc = q_ref[...]
    for g, k_ref in enumerate(page_refs):
        z_ref[g] = _head_logits(q_bc, k_ref[...])


def _sample_scores(page_table, q_bc, cache_kt):
    db, n_pages = page_table.shape
    g = SCORE_PAGES if n_pages % SCORE_PAGES == 0 else 1
    page_specs = [pl.BlockSpec((None, N_HEADS, HEAD_DIM, PAGE_SIZE),
                               functools.partial(lambda b, c, pt, j: (pt[b, c * g + j], 0, 0, 0), j=j))
                  for j in range(g)]
    return pl.pallas_call(
        _sample_scores_kernel,
        grid_spec=pltpu.PrefetchScalarGridSpec(
            num_scalar_prefetch=1,
            grid=(db, n_pages // g),
            in_specs=[pl.BlockSpec((None, N_HEADS, HEAD_DIM, LANES), lambda b, c, pt: (b, 0, 0, 0))]
            + page_specs,
            out_specs=pl.BlockSpec((None, g, N_HEADS, PAGE_SIZE), lambda b, c, pt: (b, c, 0, 0)),
        ),
        out_shape=jax.ShapeDtypeStruct((db, n_pages, N_HEADS, PAGE_SIZE), F32),
        compiler_params=_cparams(("parallel", "arbitrary")),
        name="sample_scores",
    )(page_table, q_bc, *([cache_kt] * g))


def _sample_select_kernel(z_ref, sel_ref, *, n_blocks):
    pages_per_block = MOBA_BLOCK // PAGE_SIZE
    lane = lax.broadcasted_iota(jnp.int32, (N_HEADS, LANES), 1)
    scores = jnp.full((N_HEADS, LANES), -jnp.inf, F32)
    for n in range(n_blocks):
        blk = z_ref[n * pages_per_block]
        for j in range(1, pages_per_block):
            blk = blk + z_ref[n * pages_per_block + j]
        scores = jnp.where(lane == n, jnp.sum(blk, axis=-1, keepdims=True), scores)
    cand = scores
    picks = jnp.full((N_HEADS, LANES), -1, jnp.int32)
    for r in range(MOBA_TOPK):
        best = jnp.max(cand, axis=-1, keepdims=True)
        is_best = (cand == best) & (best > -jnp.inf)
        first = jnp.min(jnp.where(is_best, lane, LANES), axis=-1, keepdims=True)
        picks = jnp.where(lane == r, jnp.where(first < LANES, first, -1), picks)
        cand = jnp.where(lane == first, -jnp.inf, cand)
    sel_ref[...] = picks


def _sample_select(z):
    db, n_pages = z.shape[:2]
    n_blocks = n_pages * PAGE_SIZE // MOBA_BLOCK
    assert n_blocks <= LANES
    return pl.pallas_call(
        functools.partial(_sample_select_kernel, n_blocks=n_blocks),
        grid=(db,),
        in_specs=[pl.BlockSpec((None, n_pages, N_HEADS, PAGE_SIZE), lambda b: (b, 0, 0, 0))],
        out_specs=pl.BlockSpec((None, N_HEADS, LANES), lambda b: (b, 0, 0)),
        out_shape=jax.ShapeDtypeStruct((db, N_HEADS, LANES), jnp.int32),
        compiler_params=_cparams(("parallel",)),
        name="sample_select",
    )(z)


def _sample_moba_kernel(pt_ref, sel_ref, slopes_ref, q_ref, kn_ref, vn_ref, *refs, past):
    n_sel = MOBA_TOPK * (MOBA_BLOCK // PAGE_SIZE)
    z_refs, v_refs, o_ref = refs[:n_sel], refs[n_sel:2 * n_sel], refs[-1]
    b = pl.program_id(0)
    h = pl.program_id(1)
    pages_per_block = MOBA_BLOCK // PAGE_SIZE
    lane = lax.broadcasted_iota(jnp.int32, (1, LANES), 1)
    slope = slopes_ref[h]
    z_new = jnp.sum(q_ref[...] * kn_ref[...], axis=0, keepdims=True)
    rows, slabs = [jnp.where(lane == 0, z_new, NEG)], [vn_ref[...]]
    for r in range(MOBA_TOPK):
        blk = sel_ref[b, h * MOBA_TOPK + r]
        for j in range(pages_per_block):
            start = (jnp.maximum(blk, 0) * pages_per_block + j) * PAGE_SIZE
            dist = (start + lane - past).astype(F32)
            zr = z_refs[r * pages_per_block + j][pl.ds(h, 1), :] + slope * dist
            rows.append(jnp.where(blk >= 0, zr, NEG))
            slabs.append(v_refs[r * pages_per_block + j][...])
    m = rows[0]
    for zr in rows[1:]:
        m = jnp.maximum(m, zr)
    m = jnp.max(m, axis=-1, keepdims=True)
    l = jnp.zeros((1, LANES), F32)
    acc = jnp.zeros((HEAD_DIM, LANES), F32)
    for zr, v_t in zip(rows, slabs):
        p = jnp.exp(zr - m)
        l = l + p
        acc = acc + v_t * p
    o_ref[...] = jnp.broadcast_to(jnp.sum(acc, axis=-1, keepdims=True)
                                  / jnp.sum(l, axis=-1, keepdims=True), (HEAD_DIM, LANES))


def _sample_moba(page_table, sel, slopes, z, q_bc, kn_bc, vn_bc, cache_vt):
    db, n_pages = page_table.shape
    pages_per_block = MOBA_BLOCK // PAGE_SIZE
    head_spec = pl.BlockSpec((None, None, HEAD_DIM, LANES), lambda b, h, pt, sl: (b, h, 0, 0))

    def page_of(b, h, pt, sl, r, j):
        return jnp.maximum(sl[b, h * MOBA_TOPK + r], 0) * pages_per_block + j

    combos = [(r, j) for r in range(MOBA_TOPK) for j in range(pages_per_block)]
    z_specs = [pl.BlockSpec((None, None, N_HEADS, PAGE_SIZE),
                            functools.partial(lambda b, h, pt, sl, r, j: (b, page_of(b, h, pt, sl, r, j), 0, 0),
                                              r=r, j=j)) for r, j in combos]
    v_specs = [pl.BlockSpec((None, None, HEAD_DIM, PAGE_SIZE),
                            functools.partial(lambda b, h, pt, sl, r, j: (pt[b, page_of(b, h, pt, sl, r, j)], h, 0, 0),
                                              r=r, j=j)) for r, j in combos]
    return pl.pallas_call(
        functools.partial(_sample_moba_kernel, past=n_pages * PAGE_SIZE),
        grid_spec=pltpu.PrefetchScalarGridSpec(
            num_scalar_prefetch=2,
            grid=(db, N_HEADS),
            in_specs=[pl.BlockSpec(memory_space=pltpu.SMEM), head_spec, head_spec, head_spec]
            + z_specs + v_specs,
            out_specs=head_spec,
        ),
        out_shape=jax.ShapeDtypeStruct((db, N_HEADS, HEAD_DIM, LANES), F32),
        compiler_params=_cparams(("parallel", "arbitrary")),
        name="sample_moba",
    )(page_table, sel, slopes, q_bc, kn_bc, vn_bc, *([z] * len(combos)), *([cache_vt] * len(combos)))


def _sb_page(q_bc, k_t, v_t, u, carry, acc_ref):
    z = _head_logits(q_bc, k_t)
    log_beta = _log_sigmoid(z)
    log_keep = log_beta - z
    a = jnp.exp(log_beta + _split_dot(log_keep, u) + carry)
    for h in range(N_HEADS):
        acc_ref[h] += v_t[h] * a[h:h + 1, :]
    return carry + jnp.sum(log_keep, axis=-1, keepdims=True)


def _sample_sb_newest_kernel(pt_ref, q_ref, u_ref, *refs):
    n = SB_NEWEST_PAGES
    k_refs, v_refs = refs[:n], refs[n:2 * n]
    o_ref, acc_ref, carry_ref = refs[2 * n:]
    acc_ref[...] = jnp.zeros_like(acc_ref)
    carry = jnp.zeros((N_HEADS, 1), F32)
    for g in range(n):
        carry = _sb_page(q_ref[...], k_refs[g][...], v_refs[g][...], u_ref[...], carry, acc_ref)
    carry_ref[...] = jnp.broadcast_to(carry, carry_ref.shape)
    o_ref[...] = jnp.broadcast_to(jnp.sum(acc_ref[...], axis=-1, keepdims=True), o_ref.shape)


def _sample_sb_newest(page_table, q_bc, upper, cache_kt, cache_vt):
    db, n_pages = page_table.shape
    n = SB_NEWEST_PAGES
    tok_spec = pl.BlockSpec((None, N_HEADS, HEAD_DIM, LANES), lambda b, pt: (b, 0, 0, 0))
    page_specs = [pl.BlockSpec((None, N_HEADS, HEAD_DIM, PAGE_SIZE),
                               functools.partial(lambda b, pt, j: (pt[b, n_pages - 1 - j], 0, 0, 0), j=j))
                  for j in range(n)]
    return pl.pallas_call(
        _sample_sb_newest_kernel,
        grid_spec=pltpu.PrefetchScalarGridSpec(
            num_scalar_prefetch=1,
            grid=(db,),
            in_specs=[tok_spec, pl.BlockSpec(memory_space=pltpu.VMEM)] + page_specs * 2,
            out_specs=[tok_spec, tok_spec,
                       pl.BlockSpec((None, N_HEADS, LANES), lambda b, pt: (b, 0, 0))],
        ),
        out_shape=[jax.ShapeDtypeStruct((db, N_HEADS, HEAD_DIM, LANES), F32),
                   jax.ShapeDtypeStruct((db, N_HEADS, HEAD_DIM, LANES), F32),
                   jax.ShapeDtypeStruct((db, N_HEADS, LANES), F32)],
        compiler_params=_cparams(("parallel",)),
        name="sample_sb_newest",
    )(page_table, q_bc, upper, *([cache_kt] * n), *([cache_vt] * n))


def _sample_sb_older_kernel(pt_ref, q_ref, u_ref, acc0_ref, carry0_ref, k_ref, v_ref, o_ref,
                            acc_ref, carry_ref):
    step = pl.program_id(1)

    @pl.when(step == 0)
    def _():
        acc_ref[...] = acc0_ref[...]
        carry_ref[...] = carry0_ref[:, :1]

    @pl.when(jnp.max(carry_ref[...]) > SB_DEAD)
    def _():
        carry_ref[...] = _sb_page(q_ref[...], k_ref[...], v_ref[...], u_ref[...], carry_ref[...],
                                  acc_ref)

    @pl.when(step == pl.num_programs(1) - 1)
    def _():
        o_ref[...] = jnp.broadcast_to(jnp.sum(acc_ref[...], axis=-1, keepdims=True), o_ref.shape)


def _sample_sb_older(page_table, q_bc, upper, acc0, carry0, cache_kt, cache_vt):
    db, n_pages = page_table.shape
    n_old = n_pages - SB_NEWEST_PAGES
    tok_spec = pl.BlockSpec((None, N_HEADS, HEAD_DIM, LANES), lambda b, p, pt: (b, 0, 0, 0))
    page_spec = pl.BlockSpec((None, N_HEADS, HEAD_DIM, PAGE_SIZE),
                             lambda b, p, pt: (pt[b, n_old - 1 - p], 0, 0, 0))
    return pl.pallas_call(
        _sample_sb_older_kernel,
        grid_spec=pltpu.PrefetchScalarGridSpec(
            num_scalar_prefetch=1,
            grid=(db, n_old),
            in_specs=[tok_spec, pl.BlockSpec(memory_space=pltpu.VMEM), tok_spec,
                      pl.BlockSpec((None, N_HEADS, LANES), lambda b, p, pt: (b, 0, 0)),
                      page_spec, page_spec],
            out_specs=tok_spec,
            scratch_shapes=[pltpu.VMEM((N_HEADS, HEAD_DIM, LANES), F32),
                            pltpu.VMEM((N_HEADS, 1), F32)],
        ),
        out_shape=jax.ShapeDtypeStruct((db, N_HEADS, HEAD_DIM, LANES), F32),
        compiler_params=_cparams(("parallel", "arbitrary")),
        name="sample_sb_older",
    )(page_table, q_bc, upper, acc0, carry0, cache_kt, cache_vt)


def _row_tile(rows, cap):
    tm = min(rows, cap)
    assert rows % tm == 0, (rows, tm)
    return tm


def kernel(x_prompt, x_sample, cache_k_moba, cache_v_moba, cache_k_sb, cache_v_sb, page_table,
           g_attn_norm, w_in, g_q_norm, g_k_norm, g_out_moba, g_out_sb, w_out,
           g_mlp_norm, w_up, w_down):
    depth = w_in.shape[0]
    assert depth == 1, "the paged caches are read with a static layer index"
    b, s, d_model = x_prompt.shape
    db, t, _ = x_sample.shape
    assert t == 1 and s % MOBA_BLOCK == 0
    n_pages = page_table.shape[1]
    assert (n_pages * PAGE_SIZE) % MOBA_BLOCK == 0 and n_pages > SB_NEWEST_PAGES

    slopes = jnp.exp2(-8.0 * jnp.arange(1, N_HEADS + 1, dtype=F32) / N_HEADS)
    hd = jnp.arange(W_GROUP) // HEAD_DIM
    head_mat = (hd[:, None] == hd[None, :]).astype(BF16)
    tq = MOBA_BLOCK
    ki = jnp.arange(tq)
    upper = (ki[:, None] > ki[None, :]).astype(BF16)

    layer = 0
    row = lambda g: g[layer].reshape(1, -1)
    gq_t = jnp.tile(g_q_norm[layer], N_HEADS).reshape(1, W_GROUP)
    gk_t = jnp.tile(g_k_norm[layer], N_HEADS).reshape(1, W_GROUP)
    w_in16 = w_in[layer].astype(BF16)
    w_out16 = w_out[layer].astype(BF16)
    w_up16 = w_up[layer].astype(BF16)
    w_down16 = w_down[layer].astype(BF16)

    xp = x_prompt.reshape(b * s, d_model)
    tm = _row_tile(b * s, 512)
    (ka, va, kb, vb, qb16, ka16, kb16, vb16, qa_t, va_t, ksum) = _project(
        xp, row(g_attn_norm), w_in16, gq_t, gk_t, head_mat, tm, True)
    kmean = ksum.reshape(b, s // MOBA_BLOCK, W_GROUP) * (1.0 / MOBA_BLOCK)
    r3 = lambda a: a.reshape(b, s, W_GROUP)
    oa = _moba_prompt(slopes, qa_t, r3(ka16), va_t, kmean)
    ob = _sb_prompt(r3(qb16), r3(kb16), r3(vb16), upper, tq)
    yp = _merge_mlp(xp, oa.reshape(b * s, W_GROUP), ob.reshape(b * s, W_GROUP),
                    row(g_out_moba), row(g_out_sb), w_out16, row(g_mlp_norm), w_up16, w_down16, tm)

    xs = x_sample.reshape(db, d_model)
    tms = _row_tile(db, 512)
    (ka_s, va_s, kb_s, vb_s, qb16_s, _, _, _, qa_s) = _project(
        xs, row(g_attn_norm), w_in16, gq_t, gk_t, head_mat, tms, False)
    page_view = lambda c: jnp.transpose(c[layer], (0, 2, 3, 1))
    lane_bc = lambda a: jnp.broadcast_to(a.reshape(db, N_HEADS, HEAD_DIM, 1).astype(F32),
                                         (db, N_HEADS, HEAD_DIM, LANES))
    qa_bc = lane_bc(qa_s * ATTN_SCALE)
    z_s = _sample_scores(page_table, qa_bc, page_view(cache_k_moba))
    sel = _sample_select(z_s)[:, :, :MOBA_TOPK].reshape(db, N_HEADS * MOBA_TOPK)
    oa_s = _sample_moba(page_table, sel, slopes, z_s, qa_bc, lane_bc(ka_s), lane_bc(va_s),
                        page_view(cache_v_moba))
    upper_page = upper[:PAGE_SIZE, :PAGE_SIZE]
    qb_bc = lane_bc(qb16_s)
    ckb, cvb = page_view(cache_k_sb), page_view(cache_v_sb)
    ob_new, acc_sb, carry_sb = _sample_sb_newest(page_table, qb_bc, upper_page, ckb, cvb)
    ob_s = lax.cond(jnp.max(carry_sb) > SB_DEAD,
                    lambda: _sample_sb_older(page_table, qb_bc, upper_page, acc_sb, carry_sb, ckb, cvb),
                    lambda: ob_new)
    ys = _merge_mlp(xs, oa_s[..., 0].reshape(db, W_GROUP), ob_s[..., 0].reshape(db, W_GROUP),
                    row(g_out_moba), row(g_out_sb), w_out16, row(g_mlp_norm), w_up16, w_down16, tms)

    p5 = lambda a: a.reshape(depth, b, s, N_HEADS, HEAD_DIM)
    s5 = lambda a: a.reshape(depth, db, t, N_HEADS, HEAD_DIM)
    return (yp.reshape(b, s, d_model), ys.reshape(db, t, d_model),
            p5(ka), p5(va), p5(kb), p5(vb), s5(ka_s), s5(va_s), s5(kb_s), s5(vb_s))
```

```python
import functools

import jax
import jax.numpy as jnp
from jax import lax
from jax.experimental import pallas as pl
from jax.experimental.pallas import tpu as pltpu

F32 = jnp.float32
BF16 = jnp.bfloat16

HEAD_DIM = 64
N_HEADS = 8
W_GROUP = N_HEADS * HEAD_DIM
LANES = 128
N_PAIRS = W_GROUP // LANES
MOBA_BLOCK = 256
MOBA_TOPK = 3
PAGE_SIZE = 128
EPS = 1e-6
ATTN_SCALE = HEAD_DIM ** -0.5
NEG = -1e30
SB_DEAD = -104.0
VMEM_LIMIT = 56 * 1024 * 1024

_NT = (((1,), (1,)), ((), ()))


def _cparams(sem):
    return pltpu.CompilerParams(dimension_semantics=sem, vmem_limit_bytes=VMEM_LIMIT)


def _split_dot(x, m):
    hi = x.astype(BF16)
    lo = (x - hi.astype(F32)).astype(BF16)
    return (jnp.dot(hi, m, preferred_element_type=F32)
            + jnp.dot(lo, m, preferred_element_type=F32))


def _log_sigmoid(z):
    return jnp.minimum(z, 0.0) - jnp.log1p(jnp.exp(-jnp.abs(z)))


def _proj_kernel(x_ref, g_ref, w_ref, gq_ref, gk_ref, hm_ref,
                 ka_ref, va_ref, kb_ref, vb_ref, qb16_ref, ka16_ref, kb16_ref, vb16_ref, *extra,
                 transposed):
    x = x_ref[...]
    xn = x * lax.rsqrt(jnp.mean(x * x, axis=-1, keepdims=True) + EPS) * g_ref[...]
    xn16 = xn.astype(BF16)

    def slab(c):
        return jnp.dot(xn16, w_ref[:, c * W_GROUP:(c + 1) * W_GROUP], preferred_element_type=F32)

    def head_norm(h, g):
        ms = _split_dot(h * h, hm_ref[...]) * (1.0 / HEAD_DIM)
        return h * lax.rsqrt(ms + EPS) * g

    qa = head_norm(slab(0), gq_ref[...])
    ka = head_norm(slab(1), gk_ref[...])
    va = slab(2)
    ka16_ref[...] = ka.astype(BF16)
    if transposed:
        ka_ref[...] = ka.T
        va_ref[...] = va.T
    else:
        ka_ref[...] = ka
        va_ref[...] = va
    if transposed:
        qat_ref, vat_ref, ksum_ref = extra
        n_sub = ka.shape[0] // MOBA_BLOCK
        qa_t = qa.T
        va_t = va_ref[...].astype(BF16)
        for j in range(n_sub):
            qat_ref[j] = qa_t[:, j * MOBA_BLOCK:(j + 1) * MOBA_BLOCK]
            vat_ref[j] = va_t[:, j * MOBA_BLOCK:(j + 1) * MOBA_BLOCK]
        ksum_ref[0] = jnp.sum(ka.reshape(n_sub, MOBA_BLOCK, W_GROUP), axis=1)
    else:
        (qa_ref,) = extra
        qa_ref[...] = qa
    qb16_ref[...] = (slab(3) * ATTN_SCALE).astype(BF16)
    kb = slab(4)
    kb_ref[...] = kb.T if transposed else kb
    kb16_ref[...] = kb.astype(BF16)
    vb = slab(5)
    vb_ref[...] = vb.T if transposed else vb
    vb16_ref[...] = vb.astype(BF16)


def _project(x2d, g_attn, w_in16, gq_t, gk_t, head_mat, tm, transposed, batch=1):
    rows, d_model = x2d.shape
    n_tiles = rows // tm
    row_spec = lambda w: pl.BlockSpec((tm, w), lambda i: (i, 0))
    whole = pl.BlockSpec(memory_space=pltpu.VMEM)
    f32_out = jax.ShapeDtypeStruct((rows, W_GROUP), F32)
    b16_out = jax.ShapeDtypeStruct((rows, W_GROUP), BF16)
    out_specs = [row_spec(W_GROUP)] * 8
    out_shape = [f32_out] * 4 + [b16_out] * 4
    if transposed:
        assert tm % MOBA_BLOCK == 0
        tiles_per_seq = rows // batch // tm
        out_specs[:4] = [pl.BlockSpec((None, W_GROUP, tm),
                                      lambda i: (i // tiles_per_seq, 0, i % tiles_per_seq))] * 4
        out_shape[:4] = [jax.ShapeDtypeStruct((batch, W_GROUP, rows // batch), F32)] * 4
        n_sub = tm // MOBA_BLOCK
        t_spec = pl.BlockSpec((n_sub, W_GROUP, MOBA_BLOCK), lambda i: (i, 0, 0))
        out_specs += [t_spec, t_spec, pl.BlockSpec((1, n_sub, W_GROUP), lambda i: (i, 0, 0))]
        out_shape += [jax.ShapeDtypeStruct((rows // MOBA_BLOCK, W_GROUP, MOBA_BLOCK), F32),
                      jax.ShapeDtypeStruct((rows // MOBA_BLOCK, W_GROUP, MOBA_BLOCK), BF16),
                      jax.ShapeDtypeStruct((n_tiles, n_sub, W_GROUP), F32)]
    else:
        out_specs += [row_spec(W_GROUP)]
        out_shape += [f32_out]
    return pl.pallas_call(
        functools.partial(_proj_kernel, transposed=transposed),
        grid=(n_tiles,),
        in_specs=[row_spec(d_model), whole, whole, whole, whole, whole],
        out_specs=out_specs,
        out_shape=out_shape,
        compiler_params=_cparams(("parallel",)),
        name="project",
    )(x2d, g_attn, w_in16, gq_t, gk_t, head_mat)


def _sb_prompt_kernel(q_ref, k_ref, v_ref, u_ref, o_ref, acc_ref, carry_ref, *, tq):
    i = pl.program_id(2)
    q2 = q_ref[0]
    lane = lax.broadcasted_iota(jnp.int32, (1, LANES), 1)
    row = lax.broadcasted_iota(jnp.int32, (tq, tq), 0)
    col = lax.broadcasted_iota(jnp.int32, (tq, tq), 1)
    below_diag = col < row
    outs = []
    for hh in range(2):
        in_head = (lane >= hh * HEAD_DIM) & (lane < (hh + 1) * HEAD_DIM)
        qm = jnp.where(in_head, q2, jnp.zeros_like(q2))

        def block(j, diag):
            start = pl.multiple_of(j * tq, tq)
            kblk = k_ref[0, pl.ds(start, tq), :]
            vblk = v_ref[0, pl.ds(start, tq), :]
            z = lax.dot_general(qm, kblk, _NT, preferred_element_type=F32)
            log_beta = _log_sigmoid(z)
            log_keep = log_beta - z
            if diag:
                log_keep = jnp.where(below_diag, log_keep, 0.0)
            later = _split_dot(log_keep, u_ref[...]) + carry_ref[...]
            a = jnp.exp(log_beta + later)
            if diag:
                a = jnp.where(below_diag, a, 0.0)
            acc_ref[...] += jnp.dot(a.astype(BF16), vblk, preferred_element_type=F32)
            carry_ref[...] += jnp.sum(log_keep, axis=-1, keepdims=True)

        acc_ref[...] = jnp.zeros_like(acc_ref)
        carry_ref[...] = jnp.zeros_like(carry_ref)
        block(i, True)

        def cond(j):
            return jnp.logical_and(j >= 0, jnp.max(carry_ref[...]) > SB_DEAD)

        def body(j):
            block(j, False)
            return j - 1

        lax.while_loop(cond, body, i - 1)
        outs.append(acc_ref[...])
    o_ref[0] = jnp.where(lane < HEAD_DIM, outs[0], outs[1])


def _sb_prompt(q16, k16, v16, upper, tq):
    b, s, _ = q16.shape
    grid = (b, N_PAIRS, s // tq)
    q_spec = pl.BlockSpec((1, tq, LANES), lambda bi, pr, i: (bi, i, pr))
    kv_spec = pl.BlockSpec((1, s, LANES), lambda bi, pr, i: (bi, 0, pr))
    return pl.pallas_call(
        functools.partial(_sb_prompt_kernel, tq=tq),
        grid=grid,
        in_specs=[q_spec, kv_spec, kv_spec, pl.BlockSpec(memory_space=pltpu.VMEM)],
        out_specs=q_spec,
        out_shape=jax.ShapeDtypeStruct((b, s, W_GROUP), F32),
        scratch_shapes=[pltpu.VMEM((tq, LANES), F32), pltpu.VMEM((tq, 1), F32)],
        compiler_params=_cparams(("parallel", "parallel", "arbitrary")),
        name="sb_prompt",
    )(q16, k16, v16, upper)


def _top_k_bias(scores, cand_mask, idx, axis):
    n_idx = scores.shape[axis]
    cand = jnp.where(cand_mask, scores, -jnp.inf)
    bias = jnp.full(scores.shape, NEG, F32)
    for _ in range(MOBA_TOPK):
        best = jnp.max(cand, axis=axis, keepdims=True)
        is_best = (cand == best) & (best > -jnp.inf)
        first = jnp.min(jnp.where(is_best, idx, n_idx), axis=axis, keepdims=True)
        pick = idx == first
        bias = jnp.where(pick, 0.0, bias)
        cand = jnp.where(pick, -jnp.inf, cand)
    return bias


def _moba_prompt_kernel(slopes_ref, qt_ref, k_ref, vt_ref, kmean_ref, o_ref,
                        acc_ref, m_ref, l_ref, bias_ref, alibi_ref, q16_ref, z_ref, p_ref):
    pr = pl.program_id(1)
    i = pl.program_id(2)
    tq = tk = MOBA_BLOCK
    n_blocks = kmean_ref.shape[1]
    qt2 = qt_ref[...]
    sub = lax.broadcasted_iota(jnp.int32, (LANES, 1), 0)
    key = lax.broadcasted_iota(jnp.int32, (tk, tq), 0)
    qry = lax.broadcasted_iota(jnp.int32, (tk, tq), 1)
    rel = (key - qry).astype(F32)
    blk_idx = lax.broadcasted_iota(jnp.int32, (n_blocks, tq), 0)
    heads = range(2)

    def k_block(n):
        return k_ref[0, pl.ds(pl.multiple_of(n * tk, tk), tk), :]

    def v_block(n, hh):
        return vt_ref[n, hh * HEAD_DIM:(hh + 1) * HEAD_DIM, :]

    k_own = k_block(i)
    k_first = k_block(0)
    for hh in heads:
        in_head = (sub >= hh * HEAD_DIM) & (sub < (hh + 1) * HEAD_DIM)
        qf = jnp.where(in_head, qt2, 0.0)
        scores = jnp.dot(kmean_ref[0], qf, preferred_element_type=F32,
                         precision=lax.Precision.HIGHEST)
        bias_ref[hh] = _top_k_bias(scores, blk_idx < i, blk_idx, 0)
        alibi = slopes_ref[2 * pr + hh] * rel
        alibi_ref[hh] = alibi
        q16 = (qf * ATTN_SCALE).astype(BF16)
        q16_ref[hh] = q16
        z_ref[0, hh] = jnp.dot(k_first, q16, preferred_element_type=F32)
        z = jnp.dot(k_own, q16, preferred_element_type=F32) + alibi
        z = jnp.where(key <= qry, z, NEG)
        m0 = jnp.max(z, axis=0, keepdims=True)
        p = jnp.exp(z - m0)
        m_ref[hh] = m0
        l_ref[hh] = jnp.sum(p, axis=0, keepdims=True)
        acc_ref[hh] = jnp.zeros((HEAD_DIM, tq), F32)
        p_ref[1, hh] = p.astype(BF16)

    def trip(n, cur):
        nxt = 1 - cur
        prev_blk = jnp.where(n == 0, i, n - 1)
        k_next = k_block(jnp.minimum(n + 1, i - 1))
        off = ((n - i) * tk).astype(F32)
        for hh in heads:
            pv_prev = jnp.dot(v_block(prev_blk, hh), p_ref[nxt, hh], preferred_element_type=F32)
            z_ref[nxt, hh] = jnp.dot(k_next, q16_ref[hh], preferred_element_type=F32)
            row = bias_ref[hh, pl.ds(n, 1), :] + slopes_ref[2 * pr + hh] * off
            z = z_ref[cur, hh] + alibi_ref[hh] + row
            m_old = m_ref[hh]
            m_new = jnp.maximum(m_old, jnp.max(z, axis=0, keepdims=True))
            alpha = jnp.exp(m_old - m_new)
            p = jnp.exp(z - m_new)
            m_ref[hh] = m_new
            l_ref[hh] = alpha * l_ref[hh] + jnp.sum(p, axis=0, keepdims=True)
            p_ref[cur, hh] = p.astype(BF16)
            acc_ref[hh] = alpha * (acc_ref[hh] + pv_prev)

    def two_trips(t, _):
        trip(2 * t, 0)
        trip(2 * t + 1, 1)
        return 0

    lax.fori_loop(0, i // 2, two_trips, 0)

    @pl.when(i % 2 == 1)
    def _():
        trip(i - 1, 0)

    last_buf = jnp.where(i == 0, 1, (i - 1) % 2)
    last_blk = jnp.where(i == 0, i, i - 1)
    outs = []
    for hh in heads:
        pv = jnp.dot(v_block(last_blk, hh), p_ref[last_buf, hh], preferred_element_type=F32)
        outs.append((acc_ref[hh] + pv) / l_ref[hh])
    o_ref[0] = jnp.concatenate(outs, axis=0).T


def _moba_prompt(slopes, qa_t, k16, va_t, kmean):
    b, s, _ = k16.shape
    tq = MOBA_BLOCK
    n_blocks = s // tq
    grid = (b, N_PAIRS, n_blocks)
    return pl.pallas_call(
        _moba_prompt_kernel,
        grid=grid,
        in_specs=[pl.BlockSpec(memory_space=pltpu.SMEM),
                  pl.BlockSpec((None, LANES, tq), lambda bi, pr, i: (bi * n_blocks + i, pr, 0)),
                  pl.BlockSpec((1, s, LANES), lambda bi, pr, i: (bi, 0, pr)),
                  pl.BlockSpec((n_blocks, LANES, tq), lambda bi, pr, i: (bi, pr, 0)),
                  pl.BlockSpec((1, n_blocks, LANES), lambda bi, pr, i: (bi, 0, pr))],
        out_specs=pl.BlockSpec((1, tq, LANES), lambda bi, pr, i: (bi, i, pr)),
        out_shape=jax.ShapeDtypeStruct((b, s, W_GROUP), F32),
        scratch_shapes=[pltpu.VMEM((2, HEAD_DIM, tq), F32), pltpu.VMEM((2, 1, tq), F32),
                        pltpu.VMEM((2, 1, tq), F32), pltpu.VMEM((2, n_blocks, tq), F32),
                        pltpu.VMEM((2, tq, tq), F32), pltpu.VMEM((2, LANES, tq), BF16),
                        pltpu.VMEM((2, 2, tq, tq), F32), pltpu.VMEM((2, 2, tq, tq), BF16)],
        compiler_params=_cparams(("parallel", "parallel", "arbitrary")),
        name="moba_prompt",
    )(slopes, qa_t, k16, va_t, kmean)


def _merge_mlp_kernel(x_ref, oa_ref, ob_ref, goa_ref, gob_ref, wout_ref, gmlp_ref, wup_ref,
                      wdown_ref, y_ref, *, ff_chunk):
    def norm(v, g):
        return v * lax.rsqrt(jnp.mean(v * v, axis=-1, keepdims=True) + EPS) * g

    na = norm(oa_ref[...], goa_ref[...]).astype(BF16)
    nb = norm(ob_ref[...], gob_ref[...]).astype(BF16)
    mixed = jnp.concatenate([na, nb], axis=-1)
    h = x_ref[...] + jnp.dot(mixed, wout_ref[...], preferred_element_type=F32)
    hn16 = norm(h, gmlp_ref[...]).astype(BF16)
    y = h
    d_ff = wup_ref.shape[1]
    for c in range(d_ff // ff_chunk):
        u = jnp.dot(hn16, wup_ref[:, c * ff_chunk:(c + 1) * ff_chunk], preferred_element_type=F32)
        act = jnp.square(jnp.maximum(u, 0.0)).astype(BF16)
        y = y + jnp.dot(act, wdown_ref[c * ff_chunk:(c + 1) * ff_chunk, :],
                        preferred_element_type=F32)
    y_ref[...] = y


def _merge_mlp(x2d, oa, ob, g_oa, g_ob, w_out16, g_mlp, w_up16, w_down16, tm):
    rows, d_model = x2d.shape
    row_spec = lambda w: pl.BlockSpec((tm, w), lambda i: (i, 0))
    whole = pl.BlockSpec(memory_space=pltpu.VMEM)
    return pl.pallas_call(
        functools.partial(_merge_mlp_kernel, ff_chunk=1024),
        grid=(rows // tm,),
        in_specs=[row_spec(d_model), row_spec(W_GROUP), row_spec(W_GROUP),
                  whole, whole, whole, whole, whole, whole],
        out_specs=row_spec(d_model),
        out_shape=jax.ShapeDtypeStruct((rows, d_model), F32),
        compiler_params=_cparams(("parallel",)),
        name="merge_mlp",
    )(x2d, oa, ob, g_oa, g_ob, w_out16, g_mlp, w_up16, w_down16)


SCORE_PAGES = 16
SB_NEWEST_PAGES = 4


def _head_logits(q_bc, k_t):
    sub = lax.broadcasted_iota(jnp.int32, (N_HEADS, LANES), 0)
    z = jnp.zeros((N_HEADS, LANES), F32)
    for h in range(N_HEADS):
        z_h = jnp.sum(q_bc[h] * k_t[h], axis=0, keepdims=True)
        z = jnp.where(sub == h, z_h, z)
    return z


def _sample_scores_kernel(pt_ref, q_ref, *refs):
    page_refs, z_ref = refs[:-1], refs[-1]
    q_bc = q_ref[...]
    for g, k_ref in enumerate(page_refs):
        z_ref[g] = _head_logits(q_bc, k_ref[...])


def _sample_scores(page_table, q_bc, cache_kt):
    db, n_pages = page_table.shape
    g = SCORE_PAGES if n_pages % SCORE_PAGES == 0 else 1
    page_specs = [pl.BlockSpec((None, N_HEADS, HEAD_DIM, PAGE_SIZE),
                               functools.partial(lambda b, c, pt, j: (pt[b, c * g + j], 0, 0, 0), j=j))
                  for j in range(g)]
    return pl.pallas_call(
        _sample_scores_kernel,
        grid_spec=pltpu.PrefetchScalarGridSpec(
            num_scalar_prefetch=1,
            grid=(db, n_pages // g),
            in_specs=[pl.BlockSpec((None, N_HEADS, HEAD_DIM, LANES), lambda b, c, pt: (b, 0, 0, 0))]
            + page_specs,
            out_specs=pl.BlockSpec((None, g, N_HEADS, PAGE_SIZE), lambda b, c, pt: (b, c, 0, 0)),
        ),
        out_shape=jax.ShapeDtypeStruct((db, n_pages, N_HEADS, PAGE_SIZE), F32),
        compiler_params=_cparams(("parallel", "arbitrary")),
        name="sample_scores",
    )(page_table, q_bc, *([cache_kt] * g))


def _sample_select_kernel(z_ref, sel_ref, *, n_blocks):
    pages_per_block = MOBA_BLOCK // PAGE_SIZE
    lane = lax.broadcasted_iota(jnp.int32, (N_HEADS, LANES), 1)
    scores = jnp.full((N_HEADS, LANES), -jnp.inf, F32)
    for n in range(n_blocks):
        blk = z_ref[n * pages_per_block]
        for j in range(1, pages_per_block):
            blk = blk + z_ref[n * pages_per_block + j]
        scores = jnp.where(lane == n, jnp.sum(blk, axis=-1, keepdims=True), scores)
    cand = scores
    picks = jnp.full((N_HEADS, LANES), -1, jnp.int32)
    for r in range(MOBA_TOPK):
        best = jnp.max(cand, axis=-1, keepdims=True)
        is_best = (cand == best) & (best > -jnp.inf)
        first = jnp.min(jnp.where(is_best, lane, LANES), axis=-1, keepdims=True)
        picks = jnp.where(lane == r, jnp.where(first < LANES, first, -1), picks)
        cand = jnp.where(lane == first, -jnp.inf, cand)
    sel_ref[...] = picks


def _sample_select(z):
    db, n_pages = z.shape[:2]
    n_blocks = n_pages * PAGE_SIZE // MOBA_BLOCK
    assert n_blocks <= LANES
    return pl.pallas_call(
        functools.partial(_sample_select_kernel, n_blocks=n_blocks),
        grid=(db,),
        in_specs=[pl.BlockSpec((None, n_pages, N_HEADS, PAGE_SIZE), lambda b: (b, 0, 0, 0))],
        out_specs=pl.BlockSpec((None, N_HEADS, LANES), lambda b: (b, 0, 0)),
        out_shape=jax.ShapeDtypeStruct((db, N_HEADS, LANES), jnp.int32),
        compiler_params=_cparams(("parallel",)),
        name="sample_select",
    )(z)


def _sample_moba_kernel(pt_ref, sel_ref, slopes_ref, q_ref, kn_ref, vn_ref, *refs, past):
    n_sel = MOBA_TOPK * (MOBA_BLOCK // PAGE_SIZE)
    z_refs, v_refs, o_ref = refs[:n_sel], refs[n_sel:2 * n_sel], refs[-1]
    b = pl.program_id(0)
    h = pl.program_id(1)
    pages_per_block = MOBA_BLOCK // PAGE_SIZE
    lane = lax.broadcasted_iota(jnp.int32, (1, LANES), 1)
    slope = slopes_ref[h]
    z_new = jnp.sum(q_ref[...] * kn_ref[...], axis=0, keepdims=True)
    rows, slabs = [jnp.where(lane == 0, z_new, NEG)], [vn_ref[...]]
    for r in range(MOBA_TOPK):
        blk = sel_ref[b, h * MOBA_TOPK + r]
        for j in range(pages_per_block):
            start = (jnp.maximum(blk, 0) * pages_per_block + j) * PAGE_SIZE
            dist = (start + lane - past).astype(F32)
            zr = z_refs[r * pages_per_block + j][pl.ds(h, 1), :] + slope * dist
            rows.append(jnp.where(blk >= 0, zr, NEG))
            slabs.append(v_refs[r * pages_per_block + j][...])
    m = rows[0]
    for zr in rows[1:]:
        m = jnp.maximum(m, zr)
    m = jnp.max(m, axis=-1, keepdims=True)
    l = jnp.zeros((1, LANES), F32)
    acc = jnp.zeros((HEAD_DIM, LANES), F32)
    for zr, v_t in zip(rows, slabs):
        p = jnp.exp(zr - m)
        l = l + p
        acc = acc + v_t * p
    o_ref[...] = jnp.broadcast_to(jnp.sum(acc, axis=-1, keepdims=True)
                                  / jnp.sum(l, axis=-1, keepdims=True), (HEAD_DIM, LANES))


def _sample_moba(page_table, sel, slopes, z, q_bc, kn_bc, vn_bc, cache_vt):
    db, n_pages = page_table.shape
    pages_per_block = MOBA_BLOCK // PAGE_SIZE
    head_spec = pl.BlockSpec((None, None, HEAD_DIM, LANES), lambda b, h, pt, sl: (b, h, 0, 0))

    def page_of(b, h, pt, sl, r, j):
        return jnp.maximum(sl[b, h * MOBA_TOPK + r], 0) * pages_per_block + j

    combos = [(r, j) for r in range(MOBA_TOPK) for j in range(pages_per_block)]
    z_specs = [pl.BlockSpec((None, None, N_HEADS, PAGE_SIZE),
                            functools.partial(lambda b, h, pt, sl, r, j: (b, page_of(b, h, pt, sl, r, j), 0, 0),
                                              r=r, j=j)) for r, j in combos]
    v_specs = [pl.BlockSpec((None, None, HEAD_DIM, PAGE_SIZE),
                            functools.partial(lambda b, h, pt, sl, r, j: (pt[b, page_of(b, h, pt, sl, r, j)], h, 0, 0),
                                              r=r, j=j)) for r, j in combos]
    return pl.pallas_call(
        functools.partial(_sample_moba_kernel, past=n_pages * PAGE_SIZE),
        grid_spec=pltpu.PrefetchScalarGridSpec(
            num_scalar_prefetch=2,
            grid=(db, N_HEADS),
            in_specs=[pl.BlockSpec(memory_space=pltpu.SMEM), head_spec, head_spec, head_spec]
            + z_specs + v_specs,
            out_specs=head_spec,
        ),
        out_shape=jax.ShapeDtypeStruct((db, N_HEADS, HEAD_DIM, LANES), F32),
        compiler_params=_cparams(("parallel", "arbitrary")),
        name="sample_moba",
    )(page_table, sel, slopes, q_bc, kn_bc, vn_bc, *([z] * len(combos)), *([cache_vt] * len(combos)))


def _sb_page(q_bc, k_t, v_t, u, carry, acc_ref):
    z = _head_logits(q_bc, k_t)
    log_beta = _log_sigmoid(z)
    log_keep = log_beta - z
    a = jnp.exp(log_beta + _split_dot(log_keep, u) + carry)
    for h in range(N_HEADS):
        acc_ref[h] += v_t[h] * a[h:h + 1, :]
    return carry + jnp.sum(log_keep, axis=-1, keepdims=True)


def _sample_sb_newest_kernel(pt_ref, q_ref, u_ref, *refs):
    n = SB_NEWEST_PAGES
    k_refs, v_refs = refs[:n], refs[n:2 * n]
    o_ref, acc_ref, carry_ref = refs[2 * n:]
    acc_ref[...] = jnp.zeros_like(acc_ref)
    carry = jnp.zeros((N_HEADS, 1), F32)
    for g in range(n):
        carry = _sb_page(q_ref[...], k_refs[g][...], v_refs[g][...], u_ref[...], carry, acc_ref)
    carry_ref[...] = jnp.broadcast_to(carry, carry_ref.shape)
    o_ref[...] = jnp.broadcast_to(jnp.sum(acc_ref[...], axis=-1, keepdims=True), o_ref.shape)


def _sample_sb_newest(page_table, q_bc, upper, cache_kt, cache_vt):
    db, n_pages = page_table.shape
    n = SB_NEWEST_PAGES
    tok_spec = pl.BlockSpec((None, N_HEADS, HEAD_DIM, LANES), lambda b, pt: (b, 0, 0, 0))
    page_specs = [pl.BlockSpec((None, N_HEADS, HEAD_DIM, PAGE_SIZE),
                               functools.partial(lambda b, pt, j: (pt[b, n_pages - 1 - j], 0, 0, 0), j=j))
                  for j in range(n)]
    return pl.pallas_call(
        _sample_sb_newest_kernel,
        grid_spec=pltpu.PrefetchScalarGridSpec(
            num_scalar_prefetch=1,
            grid=(db,),
            in_specs=[tok_spec, pl.BlockSpec(memory_space=pltpu.VMEM)] + page_specs * 2,
            out_specs=[tok_spec, tok_spec,
                       pl.BlockSpec((None, N_HEADS, LANES), lambda b, pt: (b, 0, 0))],
        ),
        out_shape=[jax.ShapeDtypeStruct((db, N_HEADS, HEAD_DIM, LANES), F32),
                   jax.ShapeDtypeStruct((db, N_HEADS, HEAD_DIM, LANES), F32),
                   jax.ShapeDtypeStruct((db, N_HEADS, LANES), F32)],
        compiler_params=_cparams(("parallel",)),
        name="sample_sb_newest",
    )(page_table, q_bc, upper, *([cache_kt] * n), *([cache_vt] * n))


def _sample_sb_older_kernel(pt_ref, q_ref, u_ref, acc0_ref, carry0_ref, k_ref, v_ref, o_ref,
                            acc_ref, carry_ref):
    step = pl.program_id(1)

    @pl.when(step == 0)
    def _():
        acc_ref[...] = acc0_ref[...]
        carry_ref[...] = carry0_ref[:, :1]

    @pl.when(jnp.max(carry_ref[...]) > SB_DEAD)
    def _():
        carry_ref[...] = _sb_page(q_ref[...], k_ref[...], v_ref[...], u_ref[...], carry_ref[...],
                                  acc_ref)

    @pl.when(step == pl.num_programs(1) - 1)
    def _():
        o_ref[...] = jnp.broadcast_to(jnp.sum(acc_ref[...], axis=-1, keepdims=True), o_ref.shape)


def _sample_sb_older(page_table, q_bc, upper, acc0, carry0, cache_kt, cache_vt):
    db, n_pages = page_table.shape
    n_old = n_pages - SB_NEWEST_PAGES
    tok_spec = pl.BlockSpec((None, N_HEADS, HEAD_DIM, LANES), lambda b, p, pt: (b, 0, 0, 0))
    page_spec = pl.BlockSpec((None, N_HEADS, HEAD_DIM, PAGE_SIZE),
                             lambda b, p, pt: (pt[b, n_old - 1 - p], 0, 0, 0))
    return pl.pallas_call(
        _sample_sb_older_kernel,
        grid_spec=pltpu.PrefetchScalarGridSpec(
            num_scalar_prefetch=1,
            grid=(db, n_old),
            in_specs=[tok_spec, pl.BlockSpec(memory_space=pltpu.VMEM), tok_spec,
                      pl.BlockSpec((None, N_HEADS, LANES), lambda b, p, pt: (b, 0, 0)),
                      page_spec, page_spec],
            out_specs=tok_spec,
            scratch_shapes=[pltpu.VMEM((N_HEADS, HEAD_DIM, LANES), F32),
                            pltpu.VMEM((N_HEADS, 1), F32)],
        ),
        out_shape=jax.ShapeDtypeStruct((db, N_HEADS, HEAD_DIM, LANES), F32),
        compiler_params=_cparams(("parallel", "arbitrary")),
        name="sample_sb_older",
    )(page_table, q_bc, upper, acc0, carry0, cache_kt, cache_vt)


def _row_tile(rows, cap):
    tm = min(rows, cap)
    assert rows % tm == 0, (rows, tm)
    return tm


def kernel(x_prompt, x_sample, cache_k_moba, cache_v_moba, cache_k_sb, cache_v_sb, page_table,
           g_attn_norm, w_in, g_q_norm, g_k_norm, g_out_moba, g_out_sb, w_out,
           g_mlp_norm, w_up, w_down):
    depth = w_in.shape[0]
    assert depth == 1, "the paged caches are read with a static layer index"
    b, s, d_model = x_prompt.shape
    db, t, _ = x_sample.shape
    assert t == 1 and s % MOBA_BLOCK == 0
    n_pages = page_table.shape[1]
    assert (n_pages * PAGE_SIZE) % MOBA_BLOCK == 0 and n_pages > SB_NEWEST_PAGES

    slopes = jnp.exp2(-8.0 * jnp.arange(1, N_HEADS + 1, dtype=F32) / N_HEADS)
    hd = jnp.arange(W_GROUP) // HEAD_DIM
    head_mat = (hd[:, None] == hd[None, :]).astype(BF16)
    tq = MOBA_BLOCK
    ki = jnp.arange(tq)
    upper = (ki[:, None] > ki[None, :]).astype(BF16)

    layer = 0
    row = lambda g: g[layer].reshape(1, -1)
    gq_t = jnp.tile(g_q_norm[layer], N_HEADS).reshape(1, W_GROUP)
    gk_t = jnp.tile(g_k_norm[layer], N_HEADS).reshape(1, W_GROUP)
    w_in16 = w_in[layer].astype(BF16)
    w_out16 = w_out[layer].astype(BF16)
    w_up16 = w_up[layer].astype(BF16)
    w_down16 = w_down[layer].astype(BF16)

    xp = x_prompt.reshape(b * s, d_model)
    tm = _row_tile(b * s, 512)
    (ka, va, kb, vb, qb16, ka16, kb16, vb16, qa_t, va_t, ksum) = _project(
        xp, row(g_attn_norm), w_in16, gq_t, gk_t, head_mat, tm, True, b)
    kmean = ksum.reshape(b, s // MOBA_BLOCK, W_GROUP) * (1.0 / MOBA_BLOCK)
    r3 = lambda a: a.reshape(b, s, W_GROUP)
    oa = _moba_prompt(slopes, qa_t, r3(ka16), va_t, kmean)
    ob = _sb_prompt(r3(qb16), r3(kb16), r3(vb16), upper, tq)
    yp = _merge_mlp(xp, oa.reshape(b * s, W_GROUP), ob.reshape(b * s, W_GROUP),
                    row(g_out_moba), row(g_out_sb), w_out16, row(g_mlp_norm), w_up16, w_down16, tm)

    xs = x_sample.reshape(db, d_model)
    tms = _row_tile(db, 512)
    (ka_s, va_s, kb_s, vb_s, qb16_s, _, _, _, qa_s) = _project(
        xs, row(g_attn_norm), w_in16, gq_t, gk_t, head_mat, tms, False)
    page_view = lambda c: jnp.transpose(c[layer], (0, 2, 3, 1))
    lane_bc = lambda a: jnp.broadcast_to(a.reshape(db, N_HEADS, HEAD_DIM, 1).astype(F32),
                                         (db, N_HEADS, HEAD_DIM, LANES))
    qa_bc = lane_bc(qa_s * ATTN_SCALE)
    z_s = _sample_scores(page_table, qa_bc, page_view(cache_k_moba))
    sel = _sample_select(z_s)[:, :, :MOBA_TOPK].reshape(db, N_HEADS * MOBA_TOPK)
    oa_s = _sample_moba(page_table, sel, slopes, z_s, qa_bc, lane_bc(ka_s), lane_bc(va_s),
                        page_view(cache_v_moba))
    upper_page = upper[:PAGE_SIZE, :PAGE_SIZE]
    qb_bc = lane_bc(qb16_s)
    ckb, cvb = page_view(cache_k_sb), page_view(cache_v_sb)
    ob_new, acc_sb, carry_sb = _sample_sb_newest(page_table, qb_bc, upper_page, ckb, cvb)
    ob_s = lax.cond(jnp.max(carry_sb) > SB_DEAD,
                    lambda: _sample_sb_older(page_table, qb_bc, upper_page, acc_sb, carry_sb, ckb, cvb),
                    lambda: ob_new)
    ys = _merge_mlp(xs, oa_s[..., 0].reshape(db, W_GROUP), ob_s[..., 0].reshape(db, W_GROUP),
                    row(g_out_moba), row(g_out_sb), w_out16, row(g_mlp_norm), w_up16, w_down16, tms)

    p5 = lambda a: jnp.transpose(a.reshape(b, N_HEADS, HEAD_DIM, s), (0, 3, 1, 2))[None]
    s5 = lambda a: a.reshape(depth, db, t, N_HEADS, HEAD_DIM)
    return (yp.reshape(b, s, d_model), ys.reshape(db, t, d_model),
            p5(ka), p5(va), p5(kb), p5(vb), s5(ka_s), s5(va_s), s5(kb_s), s5(vb_s))
```
